```python
import math
import jax, jax.numpy as jnp
from jax import lax
import numpy as np

D_MODEL = 1024
BATCH = 8
SEQ = 8192
DEPTH = 1

DA_HEADS = 8
DA_HEAD_DIM = 64
DA_WIDTH = DA_HEADS * 2 * DA_HEAD_DIM
Q_BLOCK = 128
SSD_EXPAND = 2
D_INNER = SSD_EXPAND * D_MODEL
SSD_HEAD_DIM = 64
SSD_HEADS = D_INNER // SSD_HEAD_DIM
SSD_GROUPS = 4
SSD_STATE = 128
CONV_W = 4
SSD_CHUNK = 128
CONV_DIM = D_INNER + 2 * SSD_GROUPS * SSD_STATE
N_BRANCH = 2
IN_COLS = 3 * DA_WIDTH + D_INNER + CONV_DIM + SSD_HEADS + N_BRANCH * D_MODEL
N_EXPERTS = 256
TOP_K = 8
N_EXPERT_GROUPS = 8
TOPK_GROUPS = 4
D_EXPERT = 256
ROUTED_SCALE = 2.5
MOE_BLOCK = 256
DN_ALPHA = (2.0 * DEPTH) ** 0.25
DN_BETA = (8.0 * DEPTH) ** -0.25
LN_EPS = 1e-5

kernel_name = "hybrid_diffattn_ssd_moe_deepnorm"

F32 = jnp.float32


def _layer_norm(x, g, b):
    xf = x.astype(F32)
    mu = jnp.mean(xf, -1, keepdims=True)
    var = jnp.mean(jnp.square(xf - mu), -1, keepdims=True)
    return ((xf - mu) * lax.rsqrt(var + LN_EPS) * g.astype(F32) + b.astype(F32)).astype(x.dtype)


def _rms_norm(x, w, out_dtype):
    xf = x.astype(F32)
    return (xf * lax.rsqrt(jnp.mean(xf * xf, -1, keepdims=True) + LN_EPS) * w.astype(F32)).astype(out_dtype)


def _diff_attention(q, k, v, lam):
    bsz, s = q.shape[:2]
    q = jnp.transpose(q, (3, 0, 2, 1, 4))
    k = jnp.transpose(k, (3, 0, 2, 1, 4))
    v = jnp.transpose(v, (0, 2, 1, 3))
    scale = DA_HEAD_DIM ** -0.5
    k_pos = jnp.arange(s)

    def block(i):
        start = i * Q_BLOCK
        qb = lax.dynamic_slice_in_dim(q, start, Q_BLOCK, axis=3)
        scores = jnp.einsum('mbhqd,mbhkd->mbhqk', qb, k).astype(F32) * scale
        q_pos = start + jnp.arange(Q_BLOCK)
        causal = k_pos[None, :] <= q_pos[:, None]
        p = jax.nn.softmax(jnp.where(causal, scores, -jnp.inf), axis=-1)
        attn = (p[0] - lam * p[1]).astype(v.dtype)
        return jnp.einsum('bhqk,bhkv->bhqv', attn, v)

    out = lax.map(block, jnp.arange(s // Q_BLOCK))
    return jnp.transpose(out, (1, 0, 3, 2, 4)).reshape(bsz, s, DA_HEADS, 2 * DA_HEAD_DIM)


def _causal_depthwise_conv(x, w, b):
    c = x.shape[-1]
    y = lax.conv_general_dilated(x, w[:, None, :].astype(x.dtype), window_strides=(1,),
                                 padding=[(CONV_W - 1, 0)],
                                 dimension_numbers=('NWC', 'WIO', 'NWC'), feature_group_count=c)
    return y + b.astype(x.dtype)


def _ssd_chunked(x, dt, a, b_mat, c_mat):
    bsz, s = x.shape[:2]
    nc = s // SSD_CHUNK
    r = SSD_HEADS // SSD_GROUPS
    xd = (x * dt[..., None]).reshape(bsz, nc, SSD_CHUNK, SSD_GROUPS, r, SSD_HEAD_DIM)
    da = (dt * a).reshape(bsz, nc, SSD_CHUNK, SSD_GROUPS, r)
    bc = b_mat.reshape(bsz, nc, SSD_CHUNK, SSD_GROUPS, SSD_STATE)
    cc = c_mat.reshape(bsz, nc, SSD_CHUNK, SSD_GROUPS, SSD_STATE)
    xs = tuple(jnp.moveaxis(t, 1, 0) for t in (xd, da, bc, cc))
    tri = jnp.tril(jnp.ones((SSD_CHUNK, SSD_CHUNK), dtype=bool))[None, :, :, None, None]

    def step(state, inp):
        xc, dac, bcc, ccc = inp
        a_cum = jnp.cumsum(dac, axis=1)
        seg = a_cum[:, :, None] - a_cum[:, None, :]
        decay = jnp.exp(jnp.where(tri, seg, -jnp.inf))
        cb = jnp.einsum('blgn,bsgn->blsg', ccc, bcc)
        y_diag = jnp.einsum('blsgr,bsgrp->blgrp', cb[..., None] * decay, xc)
        y_off = jnp.einsum('blgn,bgrpn->blgrp', ccc, state) * jnp.exp(a_cum)[..., None]
        a_last = a_cum[:, -1]
        w = jnp.exp(a_last[:, None] - a_cum)
        new_state = state * jnp.exp(a_last)[..., None, None] + \
            jnp.einsum('bsgn,bsgrp->bgrpn', bcc, xc * w[..., None])
        return new_state, y_diag + y_off

    state0 = jnp.zeros((bsz, SSD_GROUPS, r, SSD_HEAD_DIM, SSD_STATE), F32)
    _, y = lax.scan(step, state0, xs)
    return jnp.moveaxis(y, 0, 1).reshape(bsz, s, SSD_HEADS, SSD_HEAD_DIM)


def _hybrid_mixer(x, w_in, lq1, lk1, lq2, lk2, subln_w, conv_w, conv_b, dt_bias, a_log, d_skip,
                  ssd_norm_w, w_br_attn, w_br_ssd, w_out, lam_init):
    bsz, s, _ = x.shape
    proj = jnp.einsum('bsd,dc->bsc', x, w_in)
    splits = [int(i) for i in np.cumsum([DA_WIDTH, DA_WIDTH, DA_WIDTH, D_INNER, CONV_DIM, SSD_HEADS])]
    q, k, v, z, xbc, dt_raw, gates = jnp.split(proj, splits, axis=-1)

    q = q.reshape(bsz, s, DA_HEADS, 2, DA_HEAD_DIM)
    k = k.reshape(bsz, s, DA_HEADS, 2, DA_HEAD_DIM)
    v = v.reshape(bsz, s, DA_HEADS, 2 * DA_HEAD_DIM)
    lam = (jnp.exp(jnp.sum(lq1.astype(F32) * lk1.astype(F32)))
           - jnp.exp(jnp.sum(lq2.astype(F32) * lk2.astype(F32))) + lam_init)
    attn = _diff_attention(q, k, v, lam)
    attn = (_rms_norm(attn, subln_w, F32) * (1.0 - lam_init)).astype(x.dtype).reshape(bsz, s, DA_WIDTH)

    xbc = jax.nn.silu(_causal_depthwise_conv(xbc, conv_w, conv_b))
    xs, bm, cm = jnp.split(xbc, [D_INNER, D_INNER + SSD_GROUPS * SSD_STATE], axis=-1)
    dt = jax.nn.softplus(dt_raw.astype(F32) + dt_bias.astype(F32))
    a = -jnp.exp(a_log.astype(F32))
    xs_h = xs.reshape(bsz, s, SSD_HEADS, SSD_HEAD_DIM).astype(F32)
    y = _ssd_chunked(xs_h, dt, a,
                     bm.reshape(bsz, s, SSD_GROUPS, SSD_STATE).astype(F32),
                     cm.reshape(bsz, s, SSD_GROUPS, SSD_STATE).astype(F32))
    y = y + d_skip.astype(F32)[:, None] * xs_h
    y = y.reshape(bsz, s, D_INNER) * jax.nn.silu(z.astype(F32))
    y = _rms_norm(y.reshape(bsz, s, SSD_GROUPS, D_INNER // SSD_GROUPS),
                  ssd_norm_w.reshape(SSD_GROUPS, D_INNER // SSD_GROUPS), x.dtype).reshape(bsz, s, D_INNER)

    g_attn, g_ssd = jnp.split(jax.nn.sigmoid(gates.astype(F32)).astype(x.dtype), N_BRANCH, axis=-1)
    merged = g_attn * (attn @ w_br_attn) + g_ssd * (y @ w_br_ssd)
    return merged @ w_out


def _moe(x, w_router, router_bias, w_eg, w_eu, w_ed, w_sg, w_su, w_sd):
    bsz, s, d = x.shape
    t = bsz * s
    xf = x.reshape(t, d)
    scores = jax.nn.sigmoid(jnp.dot(xf.astype(F32), w_router.astype(F32)))
    choice = (scores + router_bias.astype(F32)).reshape(t, N_EXPERT_GROUPS, N_EXPERTS // N_EXPERT_GROUPS)
    grp_score = jnp.sum(lax.top_k(choice, 2)[0], axis=-1)
    _, grp_idx = lax.top_k(grp_score, TOPK_GROUPS)
    grp_mask = jnp.sum(jax.nn.one_hot(grp_idx, N_EXPERT_GROUPS, dtype=F32), axis=1) > 0
    masked = jnp.where(grp_mask[:, :, None], choice, -jnp.inf).reshape(t, N_EXPERTS)
    _, top_idx = lax.top_k(masked, TOP_K)
    top_w = jnp.take_along_axis(scores, top_idx, axis=1)
    top_w = top_w / jnp.sum(top_w, -1, keepdims=True) * ROUTED_SCALE

    n_assign = t * TOP_K
    flat_e = top_idx.reshape(-1)
    flat_tok = jnp.broadcast_to(jnp.arange(t)[:, None], (t, TOP_K)).reshape(-1)
    flat_w = top_w.reshape(-1)
    order = jnp.argsort(flat_e)
    se, stok, sw = flat_e[order], flat_tok[order], flat_w[order]
    counts = jnp.bincount(flat_e, length=N_EXPERTS)
    blocks_per_e = (counts + MOE_BLOCK - 1) // MOE_BLOCK
    blk_end = jnp.cumsum(blocks_per_e)
    blk_start = blk_end - blocks_per_e
    offs = jnp.cumsum(counts) - counts
    pos = blk_start[se] * MOE_BLOCK + (jnp.arange(n_assign) - offs[se])
    n_blocks = (n_assign + MOE_BLOCK - 1) // MOE_BLOCK + N_EXPERTS
    n_rows = n_blocks * MOE_BLOCK
    ptok = jnp.zeros((n_rows,), jnp.int32).at[pos].set(stok)
    pw = jnp.zeros((n_rows,), F32).at[pos].set(sw)
    blk_expert = jnp.minimum(jnp.searchsorted(blk_end, jnp.arange(n_blocks), side='right'), N_EXPERTS - 1)

    def expert_block(acc, inp):
        e, tok, w = inp
        xb = xf[tok]
        h = jax.nn.silu(xb @ w_eg[e]) * (xb @ w_eu[e])
        yb = ((h @ w_ed[e]) * w[:, None]).astype(acc.dtype)
        return acc.at[tok].add(yb), None

    routed, _ = lax.scan(expert_block, jnp.zeros_like(xf),
                         (blk_expert, ptok.reshape(n_blocks, MOE_BLOCK), pw.reshape(n_blocks, MOE_BLOCK)))
    shared = (jax.nn.silu(xf @ w_sg) * (xf @ w_su)) @ w_sd
    return (routed + shared).reshape(bsz, s, d)


def _normal(key, shape, std):
    return jax.random.normal(key, shape, F32) * std


def setup_inputs(seed: int = 0) -> dict:
    key = jax.random.key(seed)
    ks = jax.random.split(key, 32)
    L, d = DEPTH, D_MODEL
    s_in = d ** -0.5
    w_in = jnp.concatenate([
        _normal(ks[1], (L, d, 2 * DA_WIDTH), s_in),
        _normal(ks[2], (L, d, DA_WIDTH), DN_BETA * s_in),
        _normal(ks[3], (L, d, D_INNER), s_in),
        _normal(ks[4], (L, d, D_INNER), DN_BETA * s_in),
        _normal(ks[5], (L, d, 2 * SSD_GROUPS * SSD_STATE), s_in),
        _normal(ks[6], (L, d, SSD_HEADS), s_in),
        _normal(ks[7], (L, d, N_BRANCH * D_MODEL), s_in),
    ], axis=-1)
    dt0 = jnp.exp(jax.random.uniform(ks[12], (L, SSD_HEADS), F32, math.log(1e-3), math.log(1e-1)))
    return {
        "x": jax.random.normal(ks[0], (BATCH, SEQ, D_MODEL), F32),
        "w_in": w_in,
        "lambda_q1": _normal(ks[8], (L, DA_HEAD_DIM), 0.1),
        "lambda_k1": _normal(ks[9], (L, DA_HEAD_DIM), 0.1),
        "lambda_q2": _normal(ks[10], (L, DA_HEAD_DIM), 0.1),
        "lambda_k2": _normal(ks[11], (L, DA_HEAD_DIM), 0.1),
        "attn_subln_w": 1.0 + _normal(ks[13], (L, 2 * DA_HEAD_DIM), 0.02),
        "conv_w": _normal(ks[14], (L, CONV_W, CONV_DIM), CONV_W ** -0.5),
        "conv_b": _normal(ks[15], (L, CONV_DIM), 0.02),
        "dt_bias": dt0 + jnp.log(-jnp.expm1(-dt0)),
        "a_log": jnp.log(jax.random.uniform(ks[16], (L, SSD_HEADS), F32, 1.0, 16.0)),
        "d_skip": 1.0 + _normal(ks[17], (L, SSD_HEADS), 0.02),
        "ssd_norm_w": 1.0 + _normal(ks[18], (L, D_INNER), 0.02),
        "w_br_attn": _normal(ks[19], (L, DA_WIDTH, d), DN_BETA * DA_WIDTH ** -0.5),
        "w_br_ssd": _normal(ks[20], (L, D_INNER, d), DN_BETA * D_INNER ** -0.5),
        "w_out": _normal(ks[21], (L, d, d), DN_BETA * s_in),
        "ln1_g": 1.0 + _normal(ks[22], (L, d), 0.02),
        "ln1_b": _normal(ks[23], (L, d), 0.02),
        "w_router": _normal(ks[24], (L, d, N_EXPERTS), s_in),
        "router_bias": _normal(ks[25], (L, N_EXPERTS), 0.01),
        "w_exp_gate": _normal(ks[26], (L, N_EXPERTS, d, D_EXPERT), DN_BETA * s_in),
        "w_exp_up": _normal(ks[27], (L, N_EXPERTS, d, D_EXPERT), DN_BETA * s_in),
        "w_exp_down": _normal(ks[28], (L, N_EXPERTS, D_EXPERT, d), DN_BETA * D_EXPERT ** -0.5),
        "w_sh_gate": _normal(ks[29], (L, d, D_EXPERT), DN_BETA * s_in),
        "w_sh_up": _normal(ks[30], (L, d, D_EXPERT), DN_BETA * s_in),
        "w_sh_down": _normal(ks[31], (L, D_EXPERT, d), DN_BETA * D_EXPERT ** -0.5),
        "ln2_g": 1.0 + _normal(jax.random.fold_in(key, 101), (L, d), 0.02),
        "ln2_b": _normal(jax.random.fold_in(key, 102), (L, d), 0.02),
    }


def reference(x, w_in, lambda_q1, lambda_k1, lambda_q2, lambda_k2, attn_subln_w, conv_w, conv_b,
              dt_bias, a_log, d_skip, ssd_norm_w, w_br_attn, w_br_ssd, w_out, ln1_g, ln1_b,
              w_router, router_bias, w_exp_gate, w_exp_up, w_exp_down, w_sh_gate, w_sh_up, w_sh_down,
              ln2_g, ln2_b):
    h = x
    for layer in range(DEPTH):
        lam_init = 0.8 - 0.6 * math.exp(-0.3 * layer)
        y = _hybrid_mixer(h, w_in[layer], lambda_q1[layer], lambda_k1[layer], lambda_q2[layer],
                          lambda_k2[layer], attn_subln_w[layer], conv_w[layer], conv_b[layer],
                          dt_bias[layer], a_log[layer], d_skip[layer], ssd_norm_w[layer],
                          w_br_attn[layer], w_br_ssd[layer], w_out[layer], lam_init)
        h = _layer_norm(DN_ALPHA * h + y, ln1_g[layer], ln1_b[layer])
        y = _moe(h, w_router[layer], router_bias[layer], w_exp_gate[layer], w_exp_up[layer],
                 w_exp_down[layer], w_sh_gate[layer], w_sh_up[layer], w_sh_down[layer])
        h = _layer_norm(DN_ALPHA * h + y, ln2_g[layer], ln2_b[layer])
    return h
```

```python
import functools
import math

import jax
import jax.numpy as jnp
from jax import lax
from jax.experimental import pallas as pl
from jax.experimental.pallas import tpu as pltpu

F32 = jnp.float32
BF16 = jnp.bfloat16
I32 = jnp.int32

D_MODEL = 1024
DEPTH = 1
DA_HEADS = 8
DA_HEAD_DIM = 64
DA_WIDTH = DA_HEADS * 2 * DA_HEAD_DIM
D_INNER = 2048
SSD_HEAD_DIM = 64
SSD_HEADS = D_INNER // SSD_HEAD_DIM
SSD_GROUPS = 4
SSD_STATE = 128
CONV_W = 4
CONV_DIM = D_INNER + 2 * SSD_GROUPS * SSD_STATE
SSD_CHUNK = 128
N_EXPERTS = 256
TOP_K = 8
N_EXPERT_GROUPS = 8
TOPK_GROUPS = 4
D_EXPERT = 256
ROUTED_SCALE = 2.5
DN_ALPHA = (2.0 * DEPTH) ** 0.25
LN_EPS = 1e-5

LANES = 128
SUBLANES = 8
ROW_TILES = D_MODEL // LANES
NEG = -1e30
VMEM_LIMIT = 56 * 1024 * 1024

NT_DIMS = (((1,), (1,)), ((), ()))


def _silu(v):
    return v * (1.0 / (1.0 + jnp.exp(-v)))


def _sigmoid(v):
    return 1.0 / (1.0 + jnp.exp(-v))


def _layer_norm(v, g, b):
    mu = jnp.mean(v, axis=-1, keepdims=True)
    d = v - mu
    var = jnp.mean(d * d, axis=-1, keepdims=True)
    return d * lax.rsqrt(var + LN_EPS) * g + b


def _load_token_tiles(ref, n_tok, row0=0):
    return jnp.concatenate(
        [ref[pl.ds(row0 + s, n_tok, stride=ROW_TILES), :] for s in range(ROW_TILES)], axis=-1)


def _store_token_tiles(ref, val, n_tok):
    for s in range(ROW_TILES):
        ref[pl.ds(s, n_tok, stride=ROW_TILES), :] = val[:, s * LANES:(s + 1) * LANES]


def _split3(v):
    hi = v.astype(BF16)
    r1 = v - hi.astype(F32)
    mid = r1.astype(BF16)
    lo = (r1 - mid.astype(F32)).astype(BF16)
    return hi, mid, lo


def _proj_kernel(x_ref, w_ref, o_ref, xb_ref):
    @pl.when(pl.program_id(1) == 0)
    def _():
        xb_ref[...] = x_ref[...].astype(BF16)

    o_ref[...] = jnp.dot(xb_ref[...], w_ref[...], preferred_element_type=F32).astype(o_ref.dtype)


def _proj(x2d, w, out_dtype, tm, tn):
    t, k = x2d.shape
    n = w.shape[1]
    return pl.pallas_call(
        _proj_kernel,
        grid=(t // tm, n // tn),
        in_specs=[pl.BlockSpec((tm, k), lambda i, j: (i, 0)),
                  pl.BlockSpec((k, tn), lambda i, j: (0, j))],
        out_specs=pl.BlockSpec((tm, tn), lambda i, j: (i, j)),
        out_shape=jax.ShapeDtypeStruct((t, n), out_dtype),
        scratch_shapes=[pltpu.VMEM((tm, k), BF16)],
        compiler_params=pltpu.CompilerParams(dimension_semantics=("parallel", "arbitrary"),
                                             vmem_limit_bytes=VMEM_LIMIT),
        name="proj",
    )(x2d, w)


def _attn_kernel(lq1_ref, lk1_ref, lq2_ref, lk2_ref, subw_ref, q_ref, k_ref, v_ref, o_ref, qs_ref,
                 *, tq, lam_init):
    i = pl.program_id(2)
    scale = DA_HEAD_DIM ** -0.5
    q = q_ref[0].astype(F32) * scale
    lane = lax.broadcasted_iota(I32, (tq, LANES), 1)
    qs_ref[0:tq, :] = jnp.where(lane < DA_HEAD_DIM, q, 0.0).astype(BF16)
    qs_ref[tq:2 * tq, :] = jnp.where(lane >= DA_HEAD_DIM, q, 0.0).astype(BF16)
    qs = qs_ref[...]

    def step(j, carry, diagonal):
        m, l, acc = carry
        start = pl.multiple_of(j * tq, tq)
        kb = k_ref[0, pl.ds(start, tq), :]
        vb = v_ref[0, pl.ds(start, tq), :]
        s = lax.dot_general(qs, kb, NT_DIMS, preferred_element_type=F32)
        if diagonal:
            row = lax.broadcasted_iota(I32, (2 * tq, tq), 0)
            row = jnp.where(row >= tq, row - tq, row)
            col = lax.broadcasted_iota(I32, (2 * tq, tq), 1)
            s = jnp.where(col <= row, s, NEG)
        m_new = jnp.maximum(m, jnp.max(s, axis=-1, keepdims=True))
        alpha = jnp.exp(m - m_new)
        p = jnp.exp(s - m_new)
        l_new = alpha * l + jnp.sum(p, axis=-1, keepdims=True)
        acc_new = alpha * acc + jnp.dot(p.astype(BF16), vb, preferred_element_type=F32)
        return m_new, l_new, acc_new

    init = (jnp.full((2 * tq, 1), NEG, F32), jnp.zeros((2 * tq, 1), F32), jnp.zeros((2 * tq, LANES), F32))
    carry = lax.fori_loop(0, i, lambda j, c: step(j, c, False), init)
    _, l, acc = step(i, carry, True)
    o = acc * (1.0 / l)
    lam = (jnp.exp(jnp.sum(lq1_ref[...] * lk1_ref[...], axis=-1, keepdims=True))
           - jnp.exp(jnp.sum(lq2_ref[...] * lk2_ref[...], axis=-1, keepdims=True)) + lam_init)
    d = o[0:tq, :] - lam * o[tq:2 * tq, :]
    ms = jnp.mean(d * d, axis=-1, keepdims=True)
    o_ref[0] = (d * lax.rsqrt(ms + LN_EPS) * subw_ref[...] * (1.0 - lam_init)).astype(o_ref.dtype)


def _attention(qkv, lq1, lk1, lq2, lk2, subw, lam_init, tq):
    b, s, _ = qkv.shape
    vec = pl.BlockSpec((1, DA_HEAD_DIM), lambda bi, h, i: (0, 0))
    return pl.pallas_call(
        functools.partial(_attn_kernel, tq=tq, lam_init=lam_init),
        grid=(b, DA_HEADS, s // tq),
        in_specs=[vec, vec, vec, vec,
                  pl.BlockSpec((1, LANES), lambda bi, h, i: (0, 0)),
                  pl.BlockSpec((1, tq, LANES), lambda bi, h, i: (bi, i, h)),
                  pl.BlockSpec((1, s, LANES), lambda bi, h, i: (bi, 0, DA_HEADS + h)),
                  pl.BlockSpec((1, s, LANES), lambda bi, h, i: (bi, 0, 2 * DA_HEADS + h))],
        out_specs=pl.BlockSpec((1, tq, LANES), lambda bi, h, i: (bi, i, h)),
        out_shape=jax.ShapeDtypeStruct((b, s, DA_WIDTH), BF16),
        scratch_shapes=[pltpu.VMEM((2 * tq, LANES), BF16)],
        compiler_params=pltpu.CompilerParams(dimension_semantics=("parallel", "parallel", "arbitrary"),
                                             vmem_limit_bytes=VMEM_LIMIT),
        name="diff_attn",
    )(lq1, lk1, lq2, lk2, subw, qkv, qkv, qkv)


def _ssd_kernel(xbc_ref, dt_ref, z_ref, convw_ref, convb_ref, dtb_ref, alog_ref, dskip_ref, normw_ref,
                y_ref, ext_ref, state_ref, *, L):
    c = pl.program_id(1)
    halo = SUBLANES

    @pl.when(c == 0)
    def _():
        ext_ref[0:halo, :] = jnp.zeros((halo, CONV_DIM), F32)
        state_ref[...] = jnp.zeros(state_ref.shape, F32)

    ext_ref[halo:halo + L, :] = xbc_ref[0]
    acc = convb_ref[...]
    for j in range(CONV_W):
        off = halo - (CONV_W - 1) + j
        acc = acc + convw_ref[j:j + 1, :] * ext_ref[off:off + L, :]
    ext_ref[0:halo, :] = ext_ref[L:L + halo, :]
    xbc = _silu(acc)
    xs = xbc[:, 0:D_INNER]
    bm = xbc[:, D_INNER:D_INNER + SSD_GROUPS * SSD_STATE]
    cm = xbc[:, D_INNER + SSD_GROUPS * SSD_STATE:CONV_DIM]

    dt_in = dt_ref[0] + dtb_ref[...]
    dt = jnp.maximum(dt_in, 0.0) + jnp.log1p(jnp.exp(-jnp.abs(dt_in)))
    a = -jnp.exp(alog_ref[...])
    da = dt * a
    row = lax.broadcasted_iota(I32, (L, L), 0)
    col = lax.broadcasted_iota(I32, (L, L), 1)
    tri = row >= col
    tril = jnp.where(tri, 1.0, 0.0).astype(BF16)
    hi, mid, lo = _split3(da)
    a_cum = (jnp.dot(tril, hi, preferred_element_type=F32) + jnp.dot(tril, mid, preferred_element_type=F32)
             + jnp.dot(tril, lo, preferred_element_type=F32))
    a_last = a_cum[L - 1:L, :]
    a_cum_t = a_cum.T
    dt_t = dt.T
    dtw_t = (dt * jnp.exp(a_last - a_cum)).T
    ea = jnp.exp(a_cum)
    e_last = jnp.exp(a_last)
    lane = lax.broadcasted_iota(I32, (1, LANES), 1)
    first_half = lane < SSD_HEAD_DIM

    heads_per_group = SSD_HEADS // SSD_GROUPS
    pairs_per_group = heads_per_group // 2
    ys = []
    for g in range(SSD_GROUPS):
        bg = bm[:, g * SSD_STATE:(g + 1) * SSD_STATE]
        cg = cm[:, g * SSD_STATE:(g + 1) * SSD_STATE]
        cb = lax.dot_general(cg.astype(BF16), bg.astype(BF16), NT_DIMS, preferred_element_type=F32)
        bg_t = bg.T
        for pp in range(pairs_per_group):
            pair = g * pairs_per_group + pp
            x_pair = xs[:, pair * LANES:(pair + 1) * LANES].astype(BF16)
            st = state_ref[pair]
            rhs = jnp.concatenate([x_pair, st.astype(BF16)], axis=0)
            y_h, s_h, e_h = [], [], []
            for hl in range(2):
                h = g * heads_per_group + 2 * pp + hl
                seg = a_cum[:, h:h + 1] - a_cum_t[h:h + 1, :]
                decay = jnp.exp(jnp.where(tri, seg, NEG))
                m_h = (cb * decay * dt_t[h:h + 1, :]).astype(BF16)
                c_h = (cg * ea[:, h:h + 1]).astype(BF16)
                lhs = jnp.concatenate([m_h, c_h], axis=1)
                y_h.append(jnp.dot(lhs, rhs, preferred_element_type=F32))
                b_s = (bg_t * dtw_t[h:h + 1, :]).astype(BF16)
                s_h.append(jnp.dot(b_s, x_pair, preferred_element_type=F32))
                e_h.append(e_last[:, h:h + 1])
            ys.append(jnp.where(first_half, y_h[0], y_h[1]))
            state_ref[pair] = (st * jnp.where(first_half, e_h[0], e_h[1])
                               + jnp.where(first_half, s_h[0], s_h[1]))
    y = jnp.concatenate(ys, axis=1)
    y = y + dskip_ref[...] * xs
    y = y * _silu(z_ref[0])
    gw = D_INNER // SSD_GROUPS
    outs = []
    for g in range(SSD_GROUPS):
        yg = y[:, g * gw:(g + 1) * gw]
        ms = jnp.mean(yg * yg, axis=-1, keepdims=True)
        outs.append(yg * lax.rsqrt(ms + LN_EPS) * normw_ref[:, g * gw:(g + 1) * gw])
    y_ref[0] = jnp.concatenate(outs, axis=1).astype(y_ref.dtype)


def _ssd(xbc_dt, zg, conv_w, conv_b, dt_bias, a_log, d_skip, norm_w, L):
    b, s, _ = xbc_dt.shape
    const = lambda shape: pl.BlockSpec(shape, lambda bi, c: (0,) * len(shape))
    return pl.pallas_call(
        functools.partial(_ssd_kernel, L=L),
        grid=(b, s // L),
        in_specs=[pl.BlockSpec((1, L, CONV_DIM), lambda bi, c: (bi, c, 0)),
                  pl.BlockSpec((1, L, LANES), lambda bi, c: (bi, c, CONV_DIM // LANES)),
                  pl.BlockSpec((1, L, D_INNER), lambda bi, c: (bi, c, 0)),
                  const((CONV_W, CONV_DIM)), const((1, CONV_DIM)), const((1, LANES)), const((1, LANES)),
                  const((1, D_INNER)), const((1, D_INNER))],
        out_specs=pl.BlockSpec((1, L, D_INNER), lambda bi, c: (bi, c, 0)),
        out_shape=jax.ShapeDtypeStruct((b, s, D_INNER), BF16),
        scratch_shapes=[pltpu.VMEM((L + 2 * SUBLANES, CONV_DIM), F32),
                        pltpu.VMEM((SSD_HEADS // 2, SSD_STATE, LANES), F32)],
        compiler_params=pltpu.CompilerParams(dimension_semantics=("parallel", "arbitrary"),
                                             vmem_limit_bytes=VMEM_LIMIT),
        name="ssd",
    )(xbc_dt, xbc_dt, zg, conv_w, conv_b, dt_bias, a_log, d_skip, norm_w)


def _merge_kernel(attn_ref, y_ref, ga_ref, gs_ref, x_ref, wa_ref, ws_ref, wo_ref, g_ref, b_ref, o_ref, *, tm):
    a = jnp.dot(attn_ref[...], wa_ref[...], preferred_element_type=F32)
    s = jnp.dot(y_ref[...], ws_ref[...], preferred_element_type=F32)
    merged = _sigmoid(ga_ref[...]) * a + _sigmoid(gs_ref[...]) * s
    out = jnp.dot(merged.astype(BF16), wo_ref[...], preferred_element_type=F32)
    h = _layer_norm(DN_ALPHA * x_ref[...] + out, g_ref[...], b_ref[...])
    _store_token_tiles(o_ref, h, tm)


def _merge(attn2d, y2d, zg2d, x2d, wa, ws, wo, g, b, tm):
    t = x2d.shape[0]
    const = lambda shape: pl.BlockSpec(shape, lambda i: (0,) * len(shape))
    return pl.pallas_call(
        functools.partial(_merge_kernel, tm=tm),
        grid=(t // tm,),
        in_specs=[pl.BlockSpec((tm, DA_WIDTH), lambda i: (i, 0)),
                  pl.BlockSpec((tm, D_INNER), lambda i: (i, 0)),
                  pl.BlockSpec((tm, D_MODEL), lambda i: (i, D_INNER // D_MODEL)),
                  pl.BlockSpec((tm, D_MODEL), lambda i: (i, D_INNER // D_MODEL + 1)),
                  pl.BlockSpec((tm, D_MODEL), lambda i: (i, 0)),
                  const((DA_WIDTH, D_MODEL)), const((D_INNER, D_MODEL)), const((D_MODEL, D_MODEL)),
                  const((1, D_MODEL)), const((1, D_MODEL))],
        out_specs=pl.BlockSpec((tm * ROW_TILES, LANES), lambda i: (i, 0)),
        out_shape=jax.ShapeDtypeStruct((t * ROW_TILES, LANES), F32),
        compiler_params=pltpu.CompilerParams(dimension_semantics=("parallel",), vmem_limit_bytes=VMEM_LIMIT),
        name="merge_ln1",
    )(attn2d, y2d, zg2d, zg2d, x2d, wa, ws, wo, g, b)


def _router_kernel(h_ref, w_ref, bias_ref, idx_ref, wgt_ref, rank_ref, cnt_ref, carry_ref, *, tt):
    i = pl.program_id(0)

    @pl.when(i == 0)
    def _():
        carry_ref[...] = jnp.zeros(carry_ref.shape, F32)

    h = _load_token_tiles(h_ref, tt)
    h_hi = h.astype(BF16)
    h_lo = (h - h_hi.astype(F32)).astype(BF16)
    w = w_ref[...]
    w_hi = w.astype(BF16)
    w_lo = (w - w_hi.astype(F32)).astype(BF16)
    logits = (lax.dot_general(w_hi, h_hi, NT_DIMS, preferred_element_type=F32)
              + lax.dot_general(w_hi, h_lo, NT_DIMS, preferred_element_type=F32)
              + lax.dot_general(w_lo, h_hi, NT_DIMS, preferred_element_type=F32))
    scores = _sigmoid(logits)
    choice = scores + bias_ref[...]

    gsz = N_EXPERTS // N_EXPERT_GROUPS
    iota_g = lax.broadcasted_iota(I32, (gsz, tt), 0)
    rows = []
    for g in range(N_EXPERT_GROUPS):
        cg = choice[g * gsz:(g + 1) * gsz, :]
        m1 = jnp.max(cg, axis=0, keepdims=True)
        i1 = jnp.min(jnp.where(cg == m1, iota_g, gsz), axis=0, keepdims=True)
        m2 = jnp.max(jnp.where(iota_g == i1, NEG, cg), axis=0, keepdims=True)
        rows.append(m1 + m2)
    gscore = jnp.concatenate(rows, axis=0)

    iota_grp = lax.broadcasted_iota(I32, (N_EXPERT_GROUPS, tt), 0)
    sel = jnp.zeros((N_EXPERT_GROUPS, tt), F32)
    cur = gscore
    for _ in range(TOPK_GROUPS):
        m = jnp.max(cur, axis=0, keepdims=True)
        ig = jnp.min(jnp.where(cur == m, iota_grp, N_EXPERT_GROUPS), axis=0, keepdims=True)
        hit = iota_grp == ig
        sel = jnp.where(hit, 1.0, sel)
        cur = jnp.where(hit, NEG, cur)
    sel_e = jnp.concatenate([jnp.broadcast_to(sel[g:g + 1, :], (gsz, tt)) for g in range(N_EXPERT_GROUPS)], axis=0)
    cur = jnp.where(sel_e > 0.5, choice, NEG)

    iota_e = lax.broadcasted_iota(I32, (N_EXPERTS, tt), 0)
    idx_rows, w_rows, hits = [], [], []
    for _ in range(TOP_K):
        m = jnp.max(cur, axis=0, keepdims=True)
        ik = jnp.min(jnp.where(cur == m, iota_e, N_EXPERTS), axis=0, keepdims=True)
        hit = iota_e == ik
        w_rows.append(jnp.sum(jnp.where(hit, scores, 0.0), axis=0, keepdims=True))
        cur = jnp.where(hit, NEG, cur)
        idx_rows.append(ik)
        hits.append(hit)
    wv = jnp.concatenate(w_rows, axis=0)
    wv = wv / jnp.sum(wv, axis=0, keepdims=True) * ROUTED_SCALE
    idx_ref[...] = jnp.concatenate(idx_rows, axis=0)
    wgt_ref[...] = wv

    onehot = jnp.zeros((N_EXPERTS, tt), F32)
    for hit in hits:
        onehot = jnp.where(hit, 1.0, onehot)
    onehot_b = onehot.astype(BF16)
    ti = lax.broadcasted_iota(I32, (tt, tt), 0)
    tj = lax.broadcasted_iota(I32, (tt, tt), 1)
    before = jnp.where(ti < tj, 1.0, 0.0).astype(BF16)
    prefix = jnp.dot(onehot_b, before, preferred_element_type=F32) + carry_ref[...]
    rank_rows = [jnp.sum(jnp.where(hit, prefix, 0.0), axis=0, keepdims=True) for hit in hits]
    rank_ref[...] = jnp.concatenate(rank_rows, axis=0).astype(I32)
    total = jnp.dot(onehot_b, jnp.ones((tt, tt), BF16), preferred_element_type=F32)
    new_carry = carry_ref[...] + total
    carry_ref[...] = new_carry
    cnt_ref[...] = new_carry[:, 0:LANES]


def _router(h_tiles, w_router_t, bias_col, tt):
    t = h_tiles.shape[0] // ROW_TILES
    kt = pl.BlockSpec((TOP_K, tt), lambda i: (0, i))
    return pl.pallas_call(
        functools.partial(_router_kernel, tt=tt),
        grid=(t // tt,),
        in_specs=[pl.BlockSpec((tt * ROW_TILES, LANES), lambda i: (i, 0)),
                  pl.BlockSpec((N_EXPERTS, D_MODEL), lambda i: (0, 0)),
                  pl.BlockSpec((N_EXPERTS, 1), lambda i: (0, 0))],
        out_specs=[kt, kt, kt, pl.BlockSpec((N_EXPERTS, LANES), lambda i: (0, 0))],
        out_shape=[jax.ShapeDtypeStruct((TOP_K, t), I32), jax.ShapeDtypeStruct((TOP_K, t), F32),
                   jax.ShapeDtypeStruct((TOP_K, t), I32), jax.ShapeDtypeStruct((N_EXPERTS, LANES), F32)],
        scratch_shapes=[pltpu.VMEM((N_EXPERTS, tt), F32)],
        compiler_params=pltpu.CompilerParams(dimension_semantics=("arbitrary",), vmem_limit_bytes=VMEM_LIMIT),
        name="moe_router",
    )(h_tiles, w_router_t, bias_col)


def _pos_kernel(idx_ref, rank_ref, base_ref, pos_ref, *, tt):
    iota_e = lax.broadcasted_iota(I32, (N_EXPERTS, tt), 0)
    base = base_ref[...]
    rows = []
    for k in range(TOP_K):
        hit = iota_e == idx_ref[k:k + 1, :]
        rows.append(jnp.sum(jnp.where(hit, base, 0.0), axis=0, keepdims=True))
    pos_ref[...] = jnp.concatenate(rows, axis=0).astype(I32) + rank_ref[...]


def _positions(idx_t, rank_t, base_col, tt):
    t = idx_t.shape[1]
    kt = pl.BlockSpec((TOP_K, tt), lambda i: (0, i))
    return pl.pallas_call(
        functools.partial(_pos_kernel, tt=tt),
        grid=(t // tt,),
        in_specs=[kt, kt, pl.BlockSpec((N_EXPERTS, 1), lambda i: (0, 0))],
        out_specs=kt,
        out_shape=jax.ShapeDtypeStruct((TOP_K, t), I32),
        compiler_params=pltpu.CompilerParams(dimension_semantics=("parallel",)),
        name="moe_positions",
    )(idx_t, rank_t, base_col)


def _dispatch_kernel(pos_ref, h_ref, xs_ref, sem, *, tt):
    def issue(t, carry):
        src = h_ref.at[pl.ds(pl.multiple_of(t * ROW_TILES, ROW_TILES), ROW_TILES), :]
        for k in range(TOP_K):
            dst_row = pl.multiple_of(pos_ref[k, t] * ROW_TILES, ROW_TILES)
            pltpu.make_async_copy(src, xs_ref.at[pl.ds(dst_row, ROW_TILES), :], sem).start()
        return carry

    lax.fori_loop(0, tt, issue, 0)
    done = xs_ref.at[pl.ds(0, tt * TOP_K * ROW_TILES), :]
    pltpu.make_async_copy(done, done, sem).wait()


def _dispatch(pos_t, h_tiles, n_rows, tt):
    t = pos_t.shape[1]
    return pl.pallas_call(
        functools.partial(_dispatch_kernel, tt=tt),
        grid=(t // tt,),
        in_specs=[pl.BlockSpec((TOP_K, tt), lambda i: (0, i), memory_space=pltpu.SMEM),
                  pl.BlockSpec((tt * ROW_TILES, LANES), lambda i: (i, 0))],
        out_specs=pl.BlockSpec(memory_space=pl.ANY),
        out_shape=jax.ShapeDtypeStruct((n_rows * ROW_TILES, LANES), F32),
        scratch_shapes=[pltpu.SemaphoreType.DMA(())],
        compiler_params=pltpu.CompilerParams(dimension_semantics=("arbitrary",), has_side_effects=True),
        name="moe_dispatch",
    )(pos_t, h_tiles)


def _expert_kernel(be_ref, nu_ref, xs_ref, wg_ref, wu_ref, wd_ref, ys_ref, *, blk):
    @pl.when(pl.program_id(0) < nu_ref[0])
    def _():
        x = _load_token_tiles(xs_ref, blk).astype(BF16)
        g = jnp.dot(x, wg_ref[0], preferred_element_type=F32)
        u = jnp.dot(x, wu_ref[0], preferred_element_type=F32)
        hmid = (_silu(g) * u).astype(BF16)
        _store_token_tiles(ys_ref, jnp.dot(hmid, wd_ref[0], preferred_element_type=F32), blk)


def _experts(blk_expert, n_used, xs, wg, wu, wd, blk):
    n_blocks = blk_expert.shape[0]
    rows = lambda j, be, nu: (jnp.minimum(j, nu[0] - 1), 0)
    return pl.pallas_call(
        functools.partial(_expert_kernel, blk=blk),
        grid_spec=pltpu.PrefetchScalarGridSpec(
            num_scalar_prefetch=2,
            grid=(n_blocks,),
            in_specs=[pl.BlockSpec((blk * ROW_TILES, LANES), rows),
                      pl.BlockSpec((1, D_MODEL, D_EXPERT), lambda j, be, nu: (be[j], 0, 0)),
                      pl.BlockSpec((1, D_MODEL, D_EXPERT), lambda j, be, nu: (be[j], 0, 0)),
                      pl.BlockSpec((1, D_EXPERT, D_MODEL), lambda j, be, nu: (be[j], 0, 0))],
            out_specs=pl.BlockSpec((blk * ROW_TILES, LANES), rows)),
        out_shape=jax.ShapeDtypeStruct(xs.shape, F32),
        compiler_params=pltpu.CompilerParams(dimension_semantics=("arbitrary",), vmem_limit_bytes=VMEM_LIMIT),
        name="moe_experts",
    )(blk_expert, n_used, xs, wg, wu, wd)


def _combine_kernel(pos_ref, wgt_ref, h_ref, wsg_ref, wsu_ref, wsd_ref, g_ref, b_ref, ys_ref, o_ref,
                    buf_ref, sem, *, tc):
    def issue(t, carry):
        for k in range(TOP_K):
            src_row = pl.multiple_of(pos_ref[k, t] * ROW_TILES, ROW_TILES)
            dst_row = pl.multiple_of((k * tc + t) * ROW_TILES, ROW_TILES)
            pltpu.make_async_copy(ys_ref.at[pl.ds(src_row, ROW_TILES), :],
                                  buf_ref.at[pl.ds(dst_row, ROW_TILES), :], sem).start()
        return carry

    lax.fori_loop(0, tc, issue, 0)
    h = _load_token_tiles(h_ref, tc)
    hb = h.astype(BF16)
    g = jnp.dot(hb, wsg_ref[...], preferred_element_type=F32)
    u = jnp.dot(hb, wsu_ref[...], preferred_element_type=F32)
    shared = jnp.dot((_silu(g) * u).astype(BF16), wsd_ref[...], preferred_element_type=F32)
    pltpu.make_async_copy(buf_ref, buf_ref, sem).wait()
    wgt = wgt_ref[...]
    chunks = []
    for s in range(ROW_TILES):
        r = None
        for k in range(TOP_K):
            term = wgt[:, k:k + 1] * buf_ref[pl.ds(k * tc * ROW_TILES + s, tc, stride=ROW_TILES), :]
            r = term if r is None else r + term
        chunks.append(r)
    routed = jnp.concatenate(chunks, axis=-1)
    o_ref[...] = _layer_norm(DN_ALPHA * h + (routed + shared), g_ref[...], b_ref[...])


def _combine(pos_t, wgt, h_tiles, wsg, wsu, wsd, g, b, ys, tc):
    t = pos_t.shape[1]
    const = lambda shape: pl.BlockSpec(shape, lambda i: (0,) * len(shape))
    return pl.pallas_call(
        functools.partial(_combine_kernel, tc=tc),
        grid=(t // tc,),
        in_specs=[pl.BlockSpec((TOP_K, tc), lambda i: (0, i), memory_space=pltpu.SMEM),
                  pl.BlockSpec((tc, TOP_K), lambda i: (i, 0)),
                  pl.BlockSpec((tc * ROW_TILES, LANES), lambda i: (i, 0)),
                  const((D_MODEL, D_EXPERT)), const((D_MODEL, D_EXPERT)), const((D_EXPERT, D_MODEL)),
                  const((1, D_MODEL)), const((1, D_MODEL)),
                  pl.BlockSpec(memory_space=pl.ANY)],
        out_specs=pl.BlockSpec((tc, D_MODEL), lambda i: (i, 0)),
        out_shape=jax.ShapeDtypeStruct((t, D_MODEL), F32),
        scratch_shapes=[pltpu.VMEM((TOP_K * tc * ROW_TILES, LANES), F32), pltpu.SemaphoreType.DMA(())],
        compiler_params=pltpu.CompilerParams(dimension_semantics=("arbitrary",), vmem_limit_bytes=VMEM_LIMIT),
        name="moe_combine_ln2",
    )(pos_t, wgt, h_tiles, wsg, wsu, wsd, g, b, ys)


def _pick(n, pref):
    t = min(n, pref)
    assert n % t == 0, (n, t)
    return t


def _layer(x, w_in, lq1, lk1, lq2, lk2, subw, conv_w, conv_b, dt_bias, a_log, d_skip, ssd_norm_w,
           w_br_attn, w_br_ssd, w_out, ln1_g, ln1_b, w_router, router_bias, w_eg, w_eu, w_ed,
           w_sg, w_su, w_sd, ln2_g, ln2_b, lam_init):
    b, s, d = x.shape
    t = b * s
    assert d == D_MODEL and s % SSD_CHUNK == 0
    x2d = x.reshape(t, d)

    o_q, o_z, o_xbc = 0, 3 * DA_WIDTH, 3 * DA_WIDTH + D_INNER
    o_dt = o_xbc + CONV_DIM
    o_g = o_dt + SSD_HEADS
    w_qkv = w_in[:, o_q:o_z].astype(BF16)
    w_zg = jnp.concatenate([w_in[:, o_z:o_xbc], w_in[:, o_g:o_g + 2 * D_MODEL]], axis=1).astype(BF16)
    w_dt = jnp.pad(w_in[:, o_dt:o_g], ((0, 0), (0, LANES - SSD_HEADS)))
    w_xd = jnp.concatenate([w_in[:, o_xbc:o_dt], w_dt], axis=1).astype(BF16)
    tm = _pick(t, 512)
    qkv = _proj(x2d, w_qkv, BF16, tm, 1024)
    zg = _proj(x2d, w_zg, F32, tm, 1024)
    xbc_dt = _proj(x2d, w_xd, F32, tm, 640)

    attn = _attention(qkv.reshape(b, s, 3 * DA_WIDTH), lq1[None], lk1[None], lq2[None], lk2[None], subw[None],
                      lam_init, _pick(s, 512))

    pad = LANES - SSD_HEADS
    y = _ssd(xbc_dt.reshape(b, s, CONV_DIM + LANES), zg.reshape(b, s, 2 * D_INNER), conv_w, conv_b[None],
             jnp.pad(dt_bias, (0, pad))[None], jnp.pad(a_log, (0, pad))[None],
             jnp.repeat(d_skip, SSD_HEAD_DIM)[None], ssd_norm_w[None], SSD_CHUNK)

    h1 = _merge(attn.reshape(t, DA_WIDTH), y.reshape(t, D_INNER), zg, x2d, w_br_attn.astype(BF16),
                w_br_ssd.astype(BF16), w_out.astype(BF16), ln1_g[None], ln1_b[None], _pick(t, 512))

    tt = _pick(t, 512)
    idx_t, wgt_t, rank_t, cnt = _router(h1, w_router.T, router_bias[:, None], tt)
    blk = 256
    n_blocks = (t * TOP_K) // blk + N_EXPERTS
    counts = cnt[:, 0].astype(I32)
    blocks_per_e = (counts + blk - 1) // blk
    blk_end = jnp.cumsum(blocks_per_e)
    blk_start = blk_end - blocks_per_e
    n_used = blk_end[-1:]
    jblk = jnp.arange(n_blocks, dtype=I32)
    blk_expert = jnp.minimum(jnp.searchsorted(blk_end, jnp.minimum(jblk, n_used[0] - 1), side="right"),
                             N_EXPERTS - 1).astype(I32)
    base_col = (blk_start * blk).astype(F32)[:, None]
    pos_t = _positions(idx_t, rank_t, base_col, tt)

    xs = _dispatch(pos_t, h1, n_blocks * blk, tt)
    ys = _experts(blk_expert, n_used.astype(I32), xs, w_eg.astype(BF16), w_eu.astype(BF16), w_ed.astype(BF16), blk)
    out = _combine(pos_t, wgt_t.T, h1, w_sg.astype(BF16), w_su.astype(BF16), w_sd.astype(BF16),
                   ln2_g[None], ln2_b[None], ys, _pick(t, 256))
    return out.reshape(b, s, d)


def kernel(x, w_in, lambda_q1, lambda_k1, lambda_q2, lambda_k2, attn_subln_w, conv_w, conv_b, dt_bias, a_log,
           d_skip, ssd_norm_w, w_br_attn, w_br_ssd, w_out, ln1_g, ln1_b, w_router, router_bias, w_exp_gate,
           w_exp_up, w_exp_down, w_sh_gate, w_sh_up, w_sh_down, ln2_g, ln2_b):
    h = x
    for layer in range(DEPTH):
        lam_init = 0.8 - 0.6 * math.exp(-0.3 * layer)
        h = _layer(h, w_in[layer], lambda_q1[layer], lambda_k1[layer], lambda_q2[layer], lambda_k2[layer],
                   attn_subln_w[layer], conv_w[layer], conv_b[layer], dt_bias[layer], a_log[layer],
                   d_skip[layer], ssd_norm_w[layer], w_br_attn[layer], w_br_ssd[layer], w_out[layer],
                   ln1_g[layer], ln1_b[layer], w_router[layer], router_bias[layer], w_exp_gate[layer],
                   w_exp_up[layer], w_exp_down[layer], w_sh_gate[layer], w_sh_up[layer], w_sh_down[layer],
                   ln2_g[layer], ln2_b[layer], lam_init)
    return h
```

```python
import functools
import math

import jax
import jax.numpy as jnp
from jax import lax
from jax.experimental import pallas as pl
from jax.experimental.pallas import tpu as pltpu

F32 = jnp.float32
BF16 = jnp.bfloat16
I32 = jnp.int32

D_MODEL = 1024
DEPTH = 1
DA_HEADS = 8
DA_HEAD_DIM = 64
DA_WIDTH = DA_HEADS * 2 * DA_HEAD_DIM
D_INNER = 2048
SSD_HEAD_DIM = 64
SSD_HEADS = D_INNER // SSD_HEAD_DIM
SSD_GROUPS = 4
SSD_STATE = 128
CONV_W = 4
CONV_DIM = D_INNER + 2 * SSD_GROUPS * SSD_STATE
SSD_CHUNK = 128
N_EXPERTS = 256
TOP_K = 8
N_EXPERT_GROUPS = 8
TOPK_GROUPS = 4
D_EXPERT = 256
ROUTED_SCALE = 2.5
DN_ALPHA = (2.0 * DEPTH) ** 0.25
LN_EPS = 1e-5

LANES = 128
SUBLANES = 8
ROW_TILES = D_MODEL // LANES
NEG = -1e30
MOE_ROW_BLOCK = 512
VMEM_LIMIT = 56 * 1024 * 1024

NT_DIMS = (((1,), (1,)), ((), ()))


def _silu(v):
    return v * (1.0 / (1.0 + jnp.exp(-v)))


def _sigmoid(v):
    return 1.0 / (1.0 + jnp.exp(-v))


def _layer_norm(v, g, b):
    mu = jnp.mean(v, axis=-1, keepdims=True)
    d = v - mu
    var = jnp.mean(d * d, axis=-1, keepdims=True)
    return d * lax.rsqrt(var + LN_EPS) * g + b


def _load_token_tiles(ref, n_tok, row0=0):
    return jnp.concatenate(
        [ref[pl.ds(row0 + s, n_tok, stride=ROW_TILES), :] for s in range(ROW_TILES)], axis=-1)


def _store_token_tiles(ref, val, n_tok):
    for s in range(ROW_TILES):
        ref[pl.ds(s, n_tok, stride=ROW_TILES), :] = val[:, s * LANES:(s + 1) * LANES]


def _split3(v):
    hi = v.astype(BF16)
    r1 = v - hi.astype(F32)
    mid = r1.astype(BF16)
    lo = (r1 - mid.astype(F32)).astype(BF16)
    return hi, mid, lo


def _proj_kernel(x_ref, w_ref, wdt_ref, o_ref, dt_ref, xb_ref):
    @pl.when(pl.program_id(1) == 0)
    def _():
        xb = x_ref[...].astype(BF16)
        xb_ref[...] = xb
        dt_ref[...] = jnp.dot(xb, wdt_ref[...], preferred_element_type=F32)

    o_ref[...] = jnp.dot(xb_ref[...], w_ref[...], preferred_element_type=F32).astype(o_ref.dtype)


def _proj(x2d, w, w_dt, tm, tn):
    t, k = x2d.shape
    n = w.shape[1]
    return pl.pallas_call(
        _proj_kernel,
        grid=(t // tm, n // tn),
        in_specs=[pl.BlockSpec((tm, k), lambda i, j: (i, 0)),
                  pl.BlockSpec((k, tn), lambda i, j: (0, j)),
                  pl.BlockSpec((k, LANES), lambda i, j: (0, 0))],
        out_specs=[pl.BlockSpec((tm, tn), lambda i, j: (i, j)),
                   pl.BlockSpec((tm, LANES), lambda i, j: (i, 0))],
        out_shape=[jax.ShapeDtypeStruct((t, n), BF16), jax.ShapeDtypeStruct((t, LANES), F32)],
        scratch_shapes=[pltpu.VMEM((tm, k), BF16)],
        compiler_params=pltpu.CompilerParams(dimension_semantics=("parallel", "arbitrary"),
                                             vmem_limit_bytes=VMEM_LIMIT),
        name="proj",
    )(x2d, w, w_dt)


ATTN_ROW_CHUNK = 64


def _attn_kernel(lq1_ref, lk1_ref, lq2_ref, lk2_ref, subw_ref, q_ref, k_ref, v_ref, o_ref,
                 qs_ref, m_ref, l_ref, acc_ref, *, tq, lam_init):
    i = pl.program_id(2)
    rows = 2 * tq
    ch = ATTN_ROW_CHUNK
    q = q_ref[0].astype(F32) * (DA_HEAD_DIM ** -0.5 * math.log2(math.e))
    lane = lax.broadcasted_iota(I32, (tq, LANES), 1)
    qs_ref[0:tq, :] = jnp.where(lane < DA_HEAD_DIM, q, 0.0).astype(BF16)
    qs_ref[tq:rows, :] = jnp.where(lane >= DA_HEAD_DIM, q, 0.0).astype(BF16)
    m_ref[...] = jnp.full((rows, LANES), NEG, F32)
    l_ref[...] = jnp.zeros((rows, LANES), F32)
    acc_ref[...] = jnp.zeros((rows, LANES), F32)
    qs = qs_ref[...]

    def step(j, diagonal):
        start = pl.multiple_of(j * tq, tq)
        kb = k_ref[0, pl.ds(start, tq), :]
        vb = v_ref[0, pl.ds(start, tq), :]
        s = lax.dot_general(qs, kb, NT_DIMS, preferred_element_type=F32)
        ps, alphas = [], []
        for c in range(rows // ch):
            r0 = c * ch
            sc = s[r0:r0 + ch, :]
            if diagonal:
                row = (r0 % tq) + lax.broadcasted_iota(I32, (ch, tq), 0)
                col = lax.broadcasted_iota(I32, (ch, tq), 1)
                sc = jnp.where(col <= row, sc, NEG)
            m_prev = m_ref[r0:r0 + ch, :]
            m_new = jnp.maximum(m_prev, jnp.max(sc, axis=-1, keepdims=True))
            alpha = jnp.exp2(m_prev - m_new)
            p = jnp.exp2(sc - jnp.concatenate([m_new] * (tq // LANES), axis=1))
            psum = p[:, 0:LANES]
            for t in range(1, tq // LANES):
                psum = psum + p[:, t * LANES:(t + 1) * LANES]
            l_ref[r0:r0 + ch, :] = alpha * l_ref[r0:r0 + ch, :] + psum
            m_ref[r0:r0 + ch, :] = m_new
            ps.append(p.astype(BF16))
            alphas.append(alpha)
        pv = jnp.dot(jnp.concatenate(ps, axis=0), vb, preferred_element_type=F32)
        acc_ref[...] = jnp.concatenate(alphas, axis=0) * acc_ref[...] + pv

    def full_block_pair(jj, carry):
        step(2 * jj, False)
        step(2 * jj + 1, False)
        return carry

    lax.fori_loop(0, i // 2, full_block_pair, 0)

    @pl.when(i % 2 == 1)
    def _():
        step(i - 1, False)

    step(i, True)
    o = acc_ref[...] / jnp.sum(l_ref[...], axis=-1, keepdims=True)
    lam = (jnp.exp(jnp.sum(lq1_ref[...] * lk1_ref[...], axis=-1, keepdims=True))
           - jnp.exp(jnp.sum(lq2_ref[...] * lk2_ref[...], axis=-1, keepdims=True)) + lam_init)
    d = o[0:tq, :] - lam * o[tq:rows, :]
    ms = jnp.mean(d * d, axis=-1, keepdims=True)
    o_ref[0] = (d * lax.rsqrt(ms + LN_EPS) * subw_ref[...] * (1.0 - lam_init)).astype(o_ref.dtype)


def _attention(proj, col0, lq1, lk1, lq2, lk2, subw, lam_init, tq):
    b, s, _ = proj.shape
    vec = pl.BlockSpec((1, DA_HEAD_DIM), lambda bi, h, i: (0, 0))
    return pl.pallas_call(
        functools.partial(_attn_kernel, tq=tq, lam_init=lam_init),
        grid=(b, DA_HEADS, s // tq),
        in_specs=[vec, vec, vec, vec,
                  pl.BlockSpec((1, LANES), lambda bi, h, i: (0, 0)),
                  pl.BlockSpec((1, tq, LANES), lambda bi, h, i: (bi, i, col0 + h)),
                  pl.BlockSpec((1, s, LANES), lambda bi, h, i: (bi, 0, col0 + DA_HEADS + h)),
                  pl.BlockSpec((1, s, LANES), lambda bi, h, i: (bi, 0, col0 + 2 * DA_HEADS + h))],
        out_specs=pl.BlockSpec((1, tq, LANES), lambda bi, h, i: (bi, i, h)),
        out_shape=jax.ShapeDtypeStruct((b, s, DA_WIDTH), BF16),
        scratch_shapes=[pltpu.VMEM((2 * tq, LANES), BF16), pltpu.VMEM((2 * tq, LANES), F32),
                        pltpu.VMEM((2 * tq, LANES), F32), pltpu.VMEM((2 * tq, LANES), F32)],
        compiler_params=pltpu.CompilerParams(dimension_semantics=("parallel", "parallel", "arbitrary"),
                                             vmem_limit_bytes=VMEM_LIMIT),
        name="diff_attn",
    )(lq1, lk1, lq2, lk2, subw, proj, proj, proj)


def _ssd_kernel(xbc_ref, dt_ref, z_ref, convw_ref, convb_ref, dtb_ref, alog_ref, dskip_ref, normw_ref,
                y_ref, ext_ref, state_ref, *, L):
    c = pl.program_id(1)
    halo = SUBLANES

    @pl.when(c == 0)
    def _():
        ext_ref[0:halo, :] = jnp.zeros((halo, CONV_DIM), F32)
        state_ref[...] = jnp.zeros(state_ref.shape, F32)

    ext_ref[halo:halo + L, :] = xbc_ref[0].astype(F32)
    acc = convb_ref[...]
    for j in range(CONV_W):
        off = halo - (CONV_W - 1) + j
        acc = acc + convw_ref[j:j + 1, :] * ext_ref[off:off + L, :]
    ext_ref[0:halo, :] = ext_ref[L:L + halo, :]
    xbc = _silu(acc)
    xs = xbc[:, 0:D_INNER]
    bm = xbc[:, D_INNER:D_INNER + SSD_GROUPS * SSD_STATE]
    cm = xbc[:, D_INNER + SSD_GROUPS * SSD_STATE:CONV_DIM]

    dt_in = dt_ref[0] + dtb_ref[...]
    dt = jnp.maximum(dt_in, 0.0) + jnp.log1p(jnp.exp(-jnp.abs(dt_in)))
    a = -jnp.exp(alog_ref[...])
    da = dt * a
    row = lax.broadcasted_iota(I32, (L, L), 0)
    col = lax.broadcasted_iota(I32, (L, L), 1)
    tri = row >= col
    tril = jnp.where(tri, 1.0, 0.0).astype(BF16)
    hi, mid, lo = _split3(da)
    a_cum = (jnp.dot(tril, hi, preferred_element_type=F32) + jnp.dot(tril, mid, preferred_element_type=F32)
             + jnp.dot(tril, lo, preferred_element_type=F32))
    a_last = a_cum[L - 1:L, :]
    a_cum_t = a_cum.T
    dt_t = dt.T
    dtw_t = (dt * jnp.exp(a_last - a_cum)).T
    ea = jnp.exp(a_cum)
    e_last = jnp.exp(a_last)
    lane = lax.broadcasted_iota(I32, (1, LANES), 1)
    first_half = lane < SSD_HEAD_DIM

    heads_per_group = SSD_HEADS // SSD_GROUPS
    pairs_per_group = heads_per_group // 2
    ys = []
    for g in range(SSD_GROUPS):
        bg = bm[:, g * SSD_STATE:(g + 1) * SSD_STATE]
        cg = cm[:, g * SSD_STATE:(g + 1) * SSD_STATE]
        cb = lax.dot_general(cg.astype(BF16), bg.astype(BF16), NT_DIMS, preferred_element_type=F32)
        bg_t = bg.T
        for pp in range(pairs_per_group):
            pair = g * pairs_per_group + pp
            x_pair = xs[:, pair * LANES:(pair + 1) * LANES].astype(BF16)
            st = state_ref[pair]
            rhs = jnp.concatenate([x_pair, st.astype(BF16)], axis=0)
            y_h, s_h, e_h = [], [], []
            for hl in range(2):
                h = g * heads_per_group + 2 * pp + hl
                seg = a_cum[:, h:h + 1] - a_cum_t[h:h + 1, :]
                decay = jnp.exp(jnp.where(tri, seg, NEG))
                m_h = (cb * decay * dt_t[h:h + 1, :]).astype(BF16)
                c_h = (cg * ea[:, h:h + 1]).astype(BF16)
                lhs = jnp.concatenate([m_h, c_h], axis=1)
                y_h.append(jnp.dot(lhs, rhs, preferred_element_type=F32))
                b_s = (bg_t * dtw_t[h:h + 1, :]).astype(BF16)
                s_h.append(jnp.dot(b_s, x_pair, preferred_element_type=F32))
                e_h.append(e_last[:, h:h + 1])
            ys.append(jnp.where(first_half, y_h[0], y_h[1]))
            state_ref[pair] = (st * jnp.where(first_half, e_h[0], e_h[1])
                               + jnp.where(first_half, s_h[0], s_h[1]))
    y = jnp.concatenate(ys, axis=1)
    y = y + dskip_ref[...] * xs
    y = y * _silu(z_ref[0].astype(F32))
    gw = D_INNER // SSD_GROUPS
    outs = []
    for g in range(SSD_GROUPS):
        yg = y[:, g * gw:(g + 1) * gw]
        ms = jnp.mean(yg * yg, axis=-1, keepdims=True)
        outs.append(yg * lax.rsqrt(ms + LN_EPS) * normw_ref[:, g * gw:(g + 1) * gw])
    y_ref[0] = jnp.concatenate(outs, axis=1).astype(y_ref.dtype)


def _ssd(proj, dt_raw, xbc_blk, z_blk, conv_w, conv_b, dt_bias, a_log, d_skip, norm_w, L):
    b, s, _ = proj.shape
    const = lambda shape: pl.BlockSpec(shape, lambda bi, c: (0,) * len(shape))
    return pl.pallas_call(
        functools.partial(_ssd_kernel, L=L),
        grid=(b, s // L),
        in_specs=[pl.BlockSpec((1, L, CONV_DIM), lambda bi, c: (bi, c, xbc_blk)),
                  pl.BlockSpec((1, L, LANES), lambda bi, c: (bi, c, 0)),
                  pl.BlockSpec((1, L, D_INNER), lambda bi, c: (bi, c, z_blk)),
                  const((CONV_W, CONV_DIM)), const((1, CONV_DIM)), const((1, LANES)), const((1, LANES)),
                  const((1, D_INNER)), const((1, D_INNER))],
        out_specs=pl.BlockSpec((1, L, D_INNER), lambda bi, c: (bi, c, 0)),
        out_shape=jax.ShapeDtypeStruct((b, s, D_INNER), BF16),
        scratch_shapes=[pltpu.VMEM((L + 2 * SUBLANES, CONV_DIM), F32),
                        pltpu.VMEM((SSD_HEADS // 2, SSD_STATE, LANES), F32)],
        compiler_params=pltpu.CompilerParams(dimension_semantics=("parallel", "arbitrary"),
                                             vmem_limit_bytes=VMEM_LIMIT),
        name="ssd",
    )(proj, dt_raw, proj, conv_w, conv_b, dt_bias, a_log, d_skip, norm_w)


def _merge_kernel(attn_ref, y_ref, ga_ref, gs_ref, x_ref, wa_ref, ws_ref, wo_ref, g_ref, b_ref, o_ref, *, tm):
    a = jnp.dot(attn_ref[...], wa_ref[...], preferred_element_type=F32)
    s = jnp.dot(y_ref[...], ws_ref[...], preferred_element_type=F32)
    merged = _sigmoid(ga_ref[...].astype(F32)) * a + _sigmoid(gs_ref[...].astype(F32)) * s
    out = jnp.dot(merged.astype(BF16), wo_ref[...], preferred_element_type=F32)
    h = _layer_norm(DN_ALPHA * x_ref[...] + out, g_ref[...], b_ref[...])
    _store_token_tiles(o_ref, h, tm)


def _merge(attn2d, y2d, proj2d, gate_blk, x2d, wa, ws, wo, g, b, tm):
    t = x2d.shape[0]
    const = lambda shape: pl.BlockSpec(shape, lambda i: (0,) * len(shape))
    return pl.pallas_call(
        functools.partial(_merge_kernel, tm=tm),
        grid=(t // tm,),
        in_specs=[pl.BlockSpec((tm, DA_WIDTH), lambda i: (i, 0)),
                  pl.BlockSpec((tm, D_INNER), lambda i: (i, 0)),
                  pl.BlockSpec((tm, D_MODEL), lambda i: (i, gate_blk)),
                  pl.BlockSpec((tm, D_MODEL), lambda i: (i, gate_blk + 1)),
                  pl.BlockSpec((tm, D_MODEL), lambda i: (i, 0)),
                  const((DA_WIDTH, D_MODEL)), const((D_INNER, D_MODEL)), const((D_MODEL, D_MODEL)),
                  const((1, D_MODEL)), const((1, D_MODEL))],
        out_specs=pl.BlockSpec((tm * ROW_TILES, LANES), lambda i: (i, 0)),
        out_shape=jax.ShapeDtypeStruct((t * ROW_TILES, LANES), F32),
        compiler_params=pltpu.CompilerParams(dimension_semantics=("parallel",), vmem_limit_bytes=VMEM_LIMIT),
        name="merge_ln1",
    )(attn2d, y2d, proj2d, proj2d, x2d, wa, ws, wo, g, b)


def _router_kernel(h_ref, w_ref, bias_ref, idx_ref, wgt_ref, rank_ref, cnt_ref, carry_ref, *, tt):
    i = pl.program_id(0)

    @pl.when(i == 0)
    def _():
        carry_ref[...] = jnp.zeros(carry_ref.shape, F32)

    h = _load_token_tiles(h_ref, tt)
    h_hi = h.astype(BF16)
    h_lo = (h - h_hi.astype(F32)).astype(BF16)
    w = w_ref[...]
    w_hi = w.astype(BF16)
    w_lo = (w - w_hi.astype(F32)).astype(BF16)
    logits = (lax.dot_general(w_hi, h_hi, NT_DIMS, preferred_element_type=F32)
              + lax.dot_general(w_hi, h_lo, NT_DIMS, preferred_element_type=F32)
              + lax.dot_general(w_lo, h_hi, NT_DIMS, preferred_element_type=F32))
    scores = _sigmoid(logits)
    choice = scores + bias_ref[...]

    gsz = N_EXPERTS // N_EXPERT_GROUPS
    iota_g = lax.broadcasted_iota(I32, (gsz, tt), 0)
    rows = []
    for g in range(N_EXPERT_GROUPS):
        cg = choice[g * gsz:(g + 1) * gsz, :]
        m1 = jnp.max(cg, axis=0, keepdims=True)
        i1 = jnp.min(jnp.where(cg == m1, iota_g, gsz), axis=0, keepdims=True)
        m2 = jnp.max(jnp.where(iota_g == i1, NEG, cg), axis=0, keepdims=True)
        rows.append(m1 + m2)
    gscore = jnp.concatenate(rows, axis=0)

    iota_grp = lax.broadcasted_iota(I32, (N_EXPERT_GROUPS, tt), 0)
    sel = jnp.zeros((N_EXPERT_GROUPS, tt), F32)
    cur = gscore
    for _ in range(TOPK_GROUPS):
        m = jnp.max(cur, axis=0, keepdims=True)
        ig = jnp.min(jnp.where(cur == m, iota_grp, N_EXPERT_GROUPS), axis=0, keepdims=True)
        hit = iota_grp == ig
        sel = jnp.where(hit, 1.0, sel)
        cur = jnp.where(hit, NEG, cur)
    sel_e = jnp.concatenate([jnp.broadcast_to(sel[g:g + 1, :], (gsz, tt)) for g in range(N_EXPERT_GROUPS)], axis=0)
    cur = jnp.where(sel_e > 0.5, choice, NEG)

    iota_e = lax.broadcasted_iota(I32, (N_EXPERTS, tt), 0)
    idx_rows, w_rows, hits = [], [], []
    for _ in range(TOP_K):
        m = jnp.max(cur, axis=0, keepdims=True)
        ik = jnp.min(jnp.where(cur == m, iota_e, N_EXPERTS), axis=0, keepdims=True)
        hit = iota_e == ik
        w_rows.append(jnp.sum(jnp.where(hit, scores, 0.0), axis=0, keepdims=True))
        cur = jnp.where(hit, NEG, cur)
        idx_rows.append(ik)
        hits.append(hit)
    wv = jnp.concatenate(w_rows, axis=0)
    wv = wv / jnp.sum(wv, axis=0, keepdims=True) * ROUTED_SCALE
    idx_ref[...] = jnp.concatenate(idx_rows, axis=0)
    wgt_ref[...] = wv

    onehot = jnp.zeros((N_EXPERTS, tt), F32)
    for hit in hits:
        onehot = jnp.where(hit, 1.0, onehot)
    onehot_b = onehot.astype(BF16)
    ti = lax.broadcasted_iota(I32, (tt, tt), 0)
    tj = lax.broadcasted_iota(I32, (tt, tt), 1)
    before = jnp.where(ti < tj, 1.0, 0.0).astype(BF16)
    prefix = jnp.dot(onehot_b, before, preferred_element_type=F32) + carry_ref[...]
    rank_rows = [jnp.sum(jnp.where(hit, prefix, 0.0), axis=0, keepdims=True) for hit in hits]
    rank_ref[...] = jnp.concatenate(rank_rows, axis=0).astype(I32)
    total = jnp.dot(onehot_b, jnp.ones((tt, tt), BF16), preferred_element_type=F32)
    new_carry = carry_ref[...] + total
    carry_ref[...] = new_carry
    cnt_ref[...] = new_carry[:, 0:LANES]


def _router(h_tiles, w_router_t, bias_col, tt):
    t = h_tiles.shape[0] // ROW_TILES
    kt = pl.BlockSpec((TOP_K, tt), lambda i: (0, i))
    return pl.pallas_call(
        functools.partial(_router_kernel, tt=tt),
        grid=(t // tt,),
        in_specs=[pl.BlockSpec((tt * ROW_TILES, LANES), lambda i: (i, 0)),
                  pl.BlockSpec((N_EXPERTS, D_MODEL), lambda i: (0, 0)),
                  pl.BlockSpec((N_EXPERTS, 1), lambda i: (0, 0))],
        out_specs=[kt, kt, kt, pl.BlockSpec((N_EXPERTS, LANES), lambda i: (0, 0))],
        out_shape=[jax.ShapeDtypeStruct((TOP_K, t), I32), jax.ShapeDtypeStruct((TOP_K, t), F32),
                   jax.ShapeDtypeStruct((TOP_K, t), I32), jax.ShapeDtypeStruct((N_EXPERTS, LANES), F32)],
        scratch_shapes=[pltpu.VMEM((N_EXPERTS, tt), F32)],
        compiler_params=pltpu.CompilerParams(dimension_semantics=("arbitrary",), vmem_limit_bytes=VMEM_LIMIT),
        name="moe_router",
    )(h_tiles, w_router_t, bias_col)


def _pos_kernel(idx_ref, rank_ref, base_ref, pos_ref, *, tt):
    iota_e = lax.broadcasted_iota(I32, (N_EXPERTS, tt), 0)
    base = base_ref[...]
    rows = []
    for k in range(TOP_K):
        hit = iota_e == idx_ref[k:k + 1, :]
        rows.append(jnp.sum(jnp.where(hit, base, 0.0), axis=0, keepdims=True))
    pos_ref[...] = jnp.concatenate(rows, axis=0).astype(I32) + rank_ref[...]


def _positions(idx_t, rank_t, base_col, tt):
    t = idx_t.shape[1]
    kt = pl.BlockSpec((TOP_K, tt), lambda i: (0, i))
    return pl.pallas_call(
        functools.partial(_pos_kernel, tt=tt),
        grid=(t // tt,),
        in_specs=[kt, kt, pl.BlockSpec((N_EXPERTS, 1), lambda i: (0, 0))],
        out_specs=kt,
        out_shape=jax.ShapeDtypeStruct((TOP_K, t), I32),
        compiler_params=pltpu.CompilerParams(dimension_semantics=("parallel",)),
        name="moe_positions",
    )(idx_t, rank_t, base_col)


def _dispatch_kernel(pos_ref, h_ref, xs_ref, sem, *, tt):
    def issue(t, carry):
        src = h_ref.at[pl.ds(pl.multiple_of(t * ROW_TILES, ROW_TILES), ROW_TILES), :]
        for k in range(TOP_K):
            dst_row = pl.multiple_of(pos_ref[k, t] * ROW_TILES, ROW_TILES)
            pltpu.make_async_copy(src, xs_ref.at[pl.ds(dst_row, ROW_TILES), :], sem).start(priority=k % 2)
        return carry

    lax.fori_loop(0, tt, issue, 0)
    done = xs_ref.at[pl.ds(0, tt * TOP_K * ROW_TILES), :]
    pltpu.make_async_copy(done, done, sem).wait()


def _dispatch(pos_t, h_tiles, n_rows, tt):
    t = pos_t.shape[1]
    return pl.pallas_call(
        functools.partial(_dispatch_kernel, tt=tt),
        grid=(t // tt,),
        in_specs=[pl.BlockSpec((TOP_K, tt), lambda i: (0, i), memory_space=pltpu.SMEM),
                  pl.BlockSpec((tt * ROW_TILES, LANES), lambda i: (i, 0))],
        out_specs=pl.BlockSpec(memory_space=pl.ANY),
        out_shape=jax.ShapeDtypeStruct((n_rows * ROW_TILES, LANES), F32),
        scratch_shapes=[pltpu.SemaphoreType.DMA(())],
        compiler_params=pltpu.CompilerParams(dimension_semantics=("arbitrary",), has_side_effects=True),
        name="moe_dispatch",
    )(pos_t, h_tiles)


def _expert_kernel(be_ref, nv_ref, nu_ref, xs_ref, wg_ref, wu_ref, wd_ref, ys_ref, wgb_ref, wub_ref, wdb_ref,
                   *, blk):
    j = pl.program_id(0)

    @pl.when(j < nu_ref[0])
    def _():
        @pl.when((j == 0) | (be_ref[j] != be_ref[jnp.maximum(j - 1, 0)]))
        def _():
            wgb_ref[...] = wg_ref[0].astype(BF16)
            wub_ref[...] = wu_ref[0].astype(BF16)
            wdb_ref[...] = wd_ref[0].astype(BF16)

        x = _load_token_tiles(xs_ref, blk).astype(BF16)
        g = jnp.dot(x, wgb_ref[...], preferred_element_type=F32)
        u = jnp.dot(x, wub_ref[...], preferred_element_type=F32)
        valid = lax.broadcasted_iota(I32, (blk, 1), 0) < nv_ref[j]
        hmid = jnp.where(valid, _silu(g) * u, 0.0).astype(BF16)
        _store_token_tiles(ys_ref, jnp.dot(hmid, wdb_ref[...], preferred_element_type=F32), blk)


def _experts(blk_expert, blk_valid, n_used, xs, wg, wu, wd, blk):
    n_blocks = blk_expert.shape[0]
    rows = lambda j, be, nv, nu: (jnp.minimum(j, nu[0] - 1), 0)
    wsel = lambda j, be, nv, nu: (be[j], 0, 0)
    return pl.pallas_call(
        functools.partial(_expert_kernel, blk=blk),
        grid_spec=pltpu.PrefetchScalarGridSpec(
            num_scalar_prefetch=3,
            grid=(n_blocks,),
            in_specs=[pl.BlockSpec((blk * ROW_TILES, LANES), rows),
                      pl.BlockSpec((1, D_MODEL, D_EXPERT), wsel),
                      pl.BlockSpec((1, D_MODEL, D_EXPERT), wsel),
                      pl.BlockSpec((1, D_EXPERT, D_MODEL), wsel)],
            out_specs=pl.BlockSpec((blk * ROW_TILES, LANES), rows),
            scratch_shapes=[pltpu.VMEM((D_MODEL, D_EXPERT), BF16), pltpu.VMEM((D_MODEL, D_EXPERT), BF16),
                            pltpu.VMEM((D_EXPERT, D_MODEL), BF16)]),
        out_shape=jax.ShapeDtypeStruct(xs.shape, F32),
        compiler_params=pltpu.CompilerParams(dimension_semantics=("arbitrary",), vmem_limit_bytes=VMEM_LIMIT),
        name="moe_experts",
    )(blk_expert, blk_valid, n_used, xs, wg, wu, wd)


def _combine_kernel(pos_ref, wgt_ref, h_ref, wsg_ref, wsu_ref, wsd_ref, g_ref, b_ref, ys_ref, o_ref,
                    buf_ref, sem, *, tc):
    def issue(t, carry):
        for k in range(TOP_K):
            src_row = pl.multiple_of(pos_ref[k, t] * ROW_TILES, ROW_TILES)
            dst_row = pl.multiple_of((k * tc + t) * ROW_TILES, ROW_TILES)
            pltpu.make_async_copy(ys_ref.at[pl.ds(src_row, ROW_TILES), :],
                                  buf_ref.at[pl.ds(dst_row, ROW_TILES), :], sem).start(priority=k % 2)
        return carry

    lax.fori_loop(0, tc, issue, 0)
    h = _load_token_tiles(h_ref, tc)
    hb = h.astype(BF16)
    g = jnp.dot(hb, wsg_ref[...], preferred_element_type=F32)
    u = jnp.dot(hb, wsu_ref[...], preferred_element_type=F32)
    shared = jnp.dot((_silu(g) * u).astype(BF16), wsd_ref[...], preferred_element_type=F32)
    pltpu.make_async_copy(buf_ref, buf_ref, sem).wait()
    wgt = wgt_ref[...]
    chunks = []
    for s in range(ROW_TILES):
        r = None
        for k in range(TOP_K):
            term = wgt[:, k:k + 1] * buf_ref[pl.ds(k * tc * ROW_TILES + s, tc, stride=ROW_TILES), :]
            r = term if r is None else r + term
        chunks.append(r)
    routed = jnp.concatenate(chunks, axis=-1)
    o_ref[...] = _layer_norm(DN_ALPHA * h + (routed + shared), g_ref[...], b_ref[...])


def _combine(pos_t, wgt, h_tiles, wsg, wsu, wsd, g, b, ys, tc):
    t = pos_t.shape[1]
    const = lambda shape: pl.BlockSpec(shape, lambda i: (0,) * len(shape))
    return pl.pallas_call(
        functools.partial(_combine_kernel, tc=tc),
        grid=(t // tc,),
        in_specs=[pl.BlockSpec((TOP_K, tc), lambda i: (0, i), memory_space=pltpu.SMEM),
                  pl.BlockSpec((tc, TOP_K), lambda i: (i, 0)),
                  pl.BlockSpec((tc * ROW_TILES, LANES), lambda i: (i, 0)),
                  const((D_MODEL, D_EXPERT)), const((D_MODEL, D_EXPERT)), const((D_EXPERT, D_MODEL)),
                  const((1, D_MODEL)), const((1, D_MODEL)),
                  pl.BlockSpec(memory_space=pl.ANY)],
        out_specs=pl.BlockSpec((tc, D_MODEL), lambda i: (i, 0)),
        out_shape=jax.ShapeDtypeStruct((t, D_MODEL), F32),
        scratch_shapes=[pltpu.VMEM((TOP_K * tc * ROW_TILES, LANES), F32), pltpu.SemaphoreType.DMA(())],
        compiler_params=pltpu.CompilerParams(dimension_semantics=("arbitrary",), vmem_limit_bytes=VMEM_LIMIT),
        name="moe_combine_ln2",
    )(pos_t, wgt, h_tiles, wsg, wsu, wsd, g, b, ys)


def _pick(n, pref):
    t = min(n, pref)
    assert n % t == 0, (n, t)
    return t


def _layer(x, w_in, lq1, lk1, lq2, lk2, subw, conv_w, conv_b, dt_bias, a_log, d_skip, ssd_norm_w,
           w_br_attn, w_br_ssd, w_out, ln1_g, ln1_b, w_router, router_bias, w_eg, w_eu, w_ed,
           w_sg, w_su, w_sd, ln2_g, ln2_b, lam_init):
    b, s, d = x.shape
    t = b * s
    assert d == D_MODEL and s % SSD_CHUNK == 0
    x2d = x.reshape(t, d)

    o_q, o_z, o_xbc = 0, 3 * DA_WIDTH, 3 * DA_WIDTH + D_INNER
    o_dt = o_xbc + CONV_DIM
    o_g = o_dt + SSD_HEADS
    w_all = jnp.concatenate([w_in[:, o_xbc:o_dt], w_in[:, o_q:o_z], w_in[:, o_z:o_xbc],
                             w_in[:, o_g:o_g + 2 * D_MODEL]], axis=1).astype(BF16)
    w_dt = jnp.pad(w_in[:, o_dt:o_g], ((0, 0), (0, LANES - SSD_HEADS))).astype(BF16)
    xbc_blk, q_col0 = 0, CONV_DIM // LANES
    z_blk = (CONV_DIM + 3 * DA_WIDTH) // D_INNER
    gate_blk = (CONV_DIM + 3 * DA_WIDTH + D_INNER) // D_MODEL
    n_proj = w_all.shape[1]
    proj, dt_raw = _proj(x2d, w_all, w_dt, _pick(t, 1024), 1024)
    proj3 = proj.reshape(b, s, n_proj)

    attn = _attention(proj3, q_col0, lq1[None], lk1[None], lq2[None], lk2[None], subw[None],
                      lam_init, _pick(s, 512))

    pad = LANES - SSD_HEADS
    y = _ssd(proj3, dt_raw.reshape(b, s, LANES), xbc_blk, z_blk, conv_w, conv_b[None],
             jnp.pad(dt_bias, (0, pad))[None], jnp.pad(a_log, (0, pad))[None],
             jnp.repeat(d_skip, SSD_HEAD_DIM)[None], ssd_norm_w[None], SSD_CHUNK)

    h1 = _merge(attn.reshape(t, DA_WIDTH), y.reshape(t, D_INNER), proj, gate_blk, x2d, w_br_attn.astype(BF16),
                w_br_ssd.astype(BF16), w_out.astype(BF16), ln1_g[None], ln1_b[None], _pick(t, 512))

    tt = _pick(t, 512)
    idx_t, wgt_t, rank_t, cnt = _router(h1, w_router.T, router_bias[:, None], tt)
    blk = MOE_ROW_BLOCK
    n_blocks = (t * TOP_K) // blk + N_EXPERTS
    counts = cnt[:, 0].astype(I32)
    blocks_per_e = (counts + blk - 1) // blk
    blk_end = jnp.cumsum(blocks_per_e)
    blk_start = blk_end - blocks_per_e
    n_used = blk_end[-1:]
    jblk = jnp.minimum(jnp.arange(n_blocks, dtype=I32), n_used[0] - 1)
    blk_expert = jnp.minimum(jnp.searchsorted(blk_end, jblk, side="right"), N_EXPERTS - 1).astype(I32)
    blk_valid = jnp.clip(counts[blk_expert] - (jblk - blk_start[blk_expert]) * blk, 0, blk).astype(I32)
    base_col = (blk_start * blk).astype(F32)[:, None]
    pos_t = _positions(idx_t, rank_t, base_col, tt)

    xs = _dispatch(pos_t, h1, n_blocks * blk, tt)
    ys = _experts(blk_expert, blk_valid, n_used.astype(I32), xs, w_eg, w_eu, w_ed, blk)
    out = _combine(pos_t, wgt_t.T, h1, w_sg.astype(BF16), w_su.astype(BF16), w_sd.astype(BF16),
                   ln2_g[None], ln2_b[None], ys, _pick(t, 256))
    return out.reshape(b, s, d)


def kernel(x, w_in, lambda_q1, lambda_k1, lambda_q2, lambda_k2, attn_subln_w, conv_w, conv_b, dt_bias, a_log,
           d_skip, ssd_norm_w, w_br_attn, w_br_ssd, w_out, ln1_g, ln1_b, w_router, router_bias, w_exp_gate,
           w_exp_up, w_exp_down, w_sh_gate, w_sh_up, w_sh_down, ln2_g, ln2_b):
    h = x
    for layer in range(DEPTH):
        lam_init = 0.8 - 0.6 * math.exp(-0.3 * layer)
        h = _layer(h, w_in[layer], lambda_q1[layer], lambda_k1[layer], lambda_q2[layer], lambda_k2[layer],
                   attn_subln_w[layer], conv_w[layer], conv_b[layer], dt_bias[layer], a_log[layer],
                   d_skip[layer], ssd_norm_w[layer], w_br_attn[layer], w_br_ssd[layer], w_out[layer],
                   ln1_g[layer], ln1_b[layer], w_router[layer], router_bias[layer], w_exp_gate[layer],
                   w_exp_up[layer], w_exp_down[layer], w_sh_gate[layer], w_sh_up[layer], w_sh_down[layer],
                   ln2_g[layer], ln2_b[layer], lam_init)
    return h
```

```python
import functools
import math

import jax
import jax.numpy as jnp
from jax import lax
from jax.experimental import pallas as pl
from jax.experimental.pallas import tpu as pltpu

F32 = jnp.float32
BF16 = jnp.bfloat16
I32 = jnp.int32

D_MODEL = 1024
DEPTH = 1
DA_HEADS = 8
DA_HEAD_DIM = 64
DA_WIDTH = DA_HEADS * 2 * DA_HEAD_DIM
D_INNER = 2048
SSD_HEAD_DIM = 64
SSD_HEADS = D_INNER // SSD_HEAD_DIM
SSD_GROUPS = 4
SSD_STATE = 128
CONV_W = 4
CONV_DIM = D_INNER + 2 * SSD_GROUPS * SSD_STATE
SSD_CHUNK = 128
N_EXPERTS = 256
TOP_K = 8
N_EXPERT_GROUPS = 8
TOPK_GROUPS = 4
D_EXPERT = 256
ROUTED_SCALE = 2.5
DN_ALPHA = (2.0 * DEPTH) ** 0.25
LN_EPS = 1e-5

LANES = 128
SUBLANES = 8
ROW_TILES = D_MODEL // LANES
NEG = -1e30
MOE_ROW_BLOCK = 1024
VMEM_LIMIT = 56 * 1024 * 1024

NT_DIMS = (((1,), (1,)), ((), ()))


def _sigmoid(v):
    return 0.5 * jnp.tanh(0.5 * v) + 0.5


def _silu(v):
    return v * _sigmoid(v)


def _layer_norm(v, g, b):
    mu = jnp.mean(v, axis=-1, keepdims=True)
    d = v - mu
    var = jnp.mean(d * d, axis=-1, keepdims=True)
    return d * lax.rsqrt(var + LN_EPS) * g + b


def _load_token_tiles(ref, n_tok, row0=0):
    return jnp.concatenate(
        [ref[pl.ds(row0 + s, n_tok, stride=ROW_TILES), :] for s in range(ROW_TILES)], axis=-1)


def _store_token_tiles(ref, val, n_tok):
    for s in range(ROW_TILES):
        ref[pl.ds(s, n_tok, stride=ROW_TILES), :] = val[:, s * LANES:(s + 1) * LANES]


PACK_ROWS = ROW_TILES // 2
U32 = jnp.uint32


def _packed_rows(row0):
    return pl.ds(pl.multiple_of(row0, PACK_ROWS), PACK_ROWS)


def _store_packed_tokens(ref, val, n_tok):
    for q in range(PACK_ROWS):
        lo = pltpu.bitcast(val[:, (2 * q) * LANES:(2 * q + 1) * LANES].astype(BF16).astype(F32), U32)
        hi = pltpu.bitcast(val[:, (2 * q + 1) * LANES:(2 * q + 2) * LANES].astype(BF16).astype(F32), U32)
        ref[pl.ds(q, n_tok, stride=PACK_ROWS), :] = (lo >> 16) | hi


def _load_packed_tokens(ref, n_tok, tok0=0):
    chunks = []
    for q in range(PACK_ROWS):
        w = ref[pl.ds(tok0 * PACK_ROWS + q, n_tok, stride=PACK_ROWS), :]
        chunks.append(pltpu.bitcast(w << 16, F32))
        chunks.append(pltpu.bitcast(w & jnp.uint32(0xFFFF0000), F32))
    return jnp.concatenate(chunks, axis=-1)


def _split3(v):
    hi = v.astype(BF16)
    r1 = v - hi.astype(F32)
    mid = r1.astype(BF16)
    lo = (r1 - mid.astype(F32)).astype(BF16)
    return hi, mid, lo


def _proj_kernel(x_ref, w_ref, wdt_ref, o_ref, dt_ref, xb_ref):
    @pl.when(pl.program_id(1) == 0)
    def _():
        xb = x_ref[...].astype(BF16)
        xb_ref[...] = xb
        dt_ref[...] = jnp.dot(xb, wdt_ref[...], preferred_element_type=F32)

    o_ref[...] = jnp.dot(xb_ref[...], w_ref[...], preferred_element_type=F32).astype(o_ref.dtype)


def _proj(x2d, w, w_dt, tm, tn):
    t, k = x2d.shape
    n = w.shape[1]
    return pl.pallas_call(
        _proj_kernel,
        grid=(t // tm, n // tn),
        in_specs=[pl.BlockSpec((tm, k), lambda i, j: (i, 0)),
                  pl.BlockSpec((k, tn), lambda i, j: (0, j)),
                  pl.BlockSpec((k, LANES), lambda i, j: (0, 0))],
        out_specs=[pl.BlockSpec((tm, tn), lambda i, j: (i, j)),
                   pl.BlockSpec((tm, LANES), lambda i, j: (i, 0))],
        out_shape=[jax.ShapeDtypeStruct((t, n), BF16), jax.ShapeDtypeStruct((t, LANES), F32)],
        scratch_shapes=[pltpu.VMEM((tm, k), BF16)],
        compiler_params=pltpu.CompilerParams(dimension_semantics=("parallel", "arbitrary"),
                                             vmem_limit_bytes=VMEM_LIMIT),
        name="proj",
    )(x2d, w, w_dt)


ATTN_ROW_CHUNK = 64


def _attn_kernel(lq1_ref, lk1_ref, lq2_ref, lk2_ref, subw_ref, q_ref, k_ref, v_ref, o_ref,
                 qs_ref, m_ref, l_ref, acc_ref, *, tq, lam_init):
    i = pl.program_id(2)
    rows = 2 * tq
    ch = ATTN_ROW_CHUNK
    q = q_ref[0].astype(F32) * (DA_HEAD_DIM ** -0.5 * math.log2(math.e))
    lane = lax.broadcasted_iota(I32, (tq, LANES), 1)
    qs_ref[0:tq, :] = jnp.where(lane < DA_HEAD_DIM, q, 0.0).astype(BF16)
    qs_ref[tq:rows, :] = jnp.where(lane >= DA_HEAD_DIM, q, 0.0).astype(BF16)
    m_ref[...] = jnp.full((rows, LANES), NEG, F32)
    l_ref[...] = jnp.zeros((rows, LANES), F32)
    acc_ref[...] = jnp.zeros((rows, LANES), F32)
    qs = qs_ref[...]

    def step(j, diagonal):
        start = pl.multiple_of(j * tq, tq)
        kb = k_ref[0, pl.ds(start, tq), :]
        vb = v_ref[0, pl.ds(start, tq), :]
        s = lax.dot_general(qs, kb, NT_DIMS, preferred_element_type=F32)
        ps, alphas = [], []
        for c in range(rows // ch):
            r0 = c * ch
            sc = s[r0:r0 + ch, :]
            if diagonal:
                row = (r0 % tq) + lax.broadcasted_iota(I32, (ch, tq), 0)
                col = lax.broadcasted_iota(I32, (ch, tq), 1)
                sc = jnp.where(col <= row, sc, NEG)
            m_prev = m_ref[r0:r0 + ch, :]
            m_new = jnp.maximum(m_prev, jnp.max(sc, axis=-1, keepdims=True))
            alpha = jnp.exp2(m_prev - m_new)
            p = jnp.exp2(sc - jnp.concatenate([m_new] * (tq // LANES), axis=1))
            psum = p[:, 0:LANES]
            for t in range(1, tq // LANES):
                psum = psum + p[:, t * LANES:(t + 1) * LANES]
            l_ref[r0:r0 + ch, :] = alpha * l_ref[r0:r0 + ch, :] + psum
            m_ref[r0:r0 + ch, :] = m_new
            ps.append(p.astype(BF16))
            alphas.append(alpha)
        pv = jnp.dot(jnp.concatenate(ps, axis=0), vb, preferred_element_type=F32)
        acc_ref[...] = jnp.concatenate(alphas, axis=0) * acc_ref[...] + pv

    def full_block_pair(jj, carry):
        step(2 * jj, False)
        step(2 * jj + 1, False)
        return carry

    lax.fori_loop(0, i // 2, full_block_pair, 0)

    @pl.when(i % 2 == 1)
    def _():
        step(i - 1, False)

    step(i, True)
    o = acc_ref[...] / jnp.sum(l_ref[...], axis=-1, keepdims=True)
    lam = (jnp.exp(jnp.sum(lq1_ref[...] * lk1_ref[...], axis=-1, keepdims=True))
           - jnp.exp(jnp.sum(lq2_ref[...] * lk2_ref[...], axis=-1, keepdims=True)) + lam_init)
    d = o[0:tq, :] - lam * o[tq:rows, :]
    ms = jnp.mean(d * d, axis=-1, keepdims=True)
    o_ref[0] = (d * lax.rsqrt(ms + LN_EPS) * subw_ref[...] * (1.0 - lam_init)).astype(o_ref.dtype)


def _attention(proj, col0, lq1, lk1, lq2, lk2, subw, lam_init, tq):
    b, s, _ = proj.shape
    vec = pl.BlockSpec((1, DA_HEAD_DIM), lambda bi, h, i: (0, 0))
    return pl.pallas_call(
        functools.partial(_attn_kernel, tq=tq, lam_init=lam_init),
        grid=(b, DA_HEADS, s // tq),
        in_specs=[vec, vec, vec, vec,
                  pl.BlockSpec((1, LANES), lambda bi, h, i: (0, 0)),
                  pl.BlockSpec((1, tq, LANES), lambda bi, h, i: (bi, i, col0 + h)),
                  pl.BlockSpec((1, s, LANES), lambda bi, h, i: (bi, 0, col0 + DA_HEADS + h)),
                  pl.BlockSpec((1, s, LANES), lambda bi, h, i: (bi, 0, col0 + 2 * DA_HEADS + h))],
        out_specs=pl.BlockSpec((1, tq, LANES), lambda bi, h, i: (bi, i, h)),
        out_shape=jax.ShapeDtypeStruct((b, s, DA_WIDTH), BF16),
        scratch_shapes=[pltpu.VMEM((2 * tq, LANES), BF16), pltpu.VMEM((2 * tq, LANES), F32),
                        pltpu.VMEM((2 * tq, LANES), F32), pltpu.VMEM((2 * tq, LANES), F32)],
        compiler_params=pltpu.CompilerParams(dimension_semantics=("parallel", "parallel", "arbitrary"),
                                             vmem_limit_bytes=VMEM_LIMIT),
        name="diff_attn",
    )(lq1, lk1, lq2, lk2, subw, proj, proj, proj)


def _ssd_kernel(xbc_ref, dt_ref, z_ref, convw_ref, convb_ref, dtb_ref, alog_ref, dskip_ref, normw_ref,
                y_ref, ext_ref, state_ref, *, L):
    c = pl.program_id(1)
    halo = SUBLANES

    @pl.when(c == 0)
    def _():
        ext_ref[0:halo, :] = jnp.zeros((halo, CONV_DIM), F32)
        state_ref[...] = jnp.zeros(state_ref.shape, F32)

    ext_ref[halo:halo + L, :] = xbc_ref[0].astype(F32)
    acc = convb_ref[...]
    for j in range(CONV_W):
        off = halo - (CONV_W - 1) + j
        acc = acc + convw_ref[j:j + 1, :] * ext_ref[off:off + L, :]
    ext_ref[0:halo, :] = ext_ref[L:L + halo, :]
    xbc = _silu(acc)
    xs = xbc[:, 0:D_INNER]
    bm = xbc[:, D_INNER:D_INNER + SSD_GROUPS * SSD_STATE]
    cm = xbc[:, D_INNER + SSD_GROUPS * SSD_STATE:CONV_DIM]

    dt_in = dt_ref[0] + dtb_ref[...]
    dt = jnp.maximum(dt_in, 0.0) + jnp.log1p(jnp.exp(-jnp.abs(dt_in)))
    a = -jnp.exp(alog_ref[...])
    da = dt * a
    row = lax.broadcasted_iota(I32, (L, L), 0)
    col = lax.broadcasted_iota(I32, (L, L), 1)
    tri = row >= col
    tril = jnp.where(tri, 1.0, 0.0).astype(BF16)
    hi, mid, lo = _split3(da)
    a_cum = (jnp.dot(tril, hi, preferred_element_type=F32) + jnp.dot(tril, mid, preferred_element_type=F32)
             + jnp.dot(tril, lo, preferred_element_type=F32))
    a_last = a_cum[L - 1:L, :]
    a_cum_t = a_cum.T
    dt_t = dt.T
    dtw_t = (dt * jnp.exp(a_last - a_cum)).T
    ea = jnp.exp(a_cum)
    e_last = jnp.exp(a_last)
    lane = lax.broadcasted_iota(I32, (1, LANES), 1)
    first_half = lane < SSD_HEAD_DIM

    heads_per_group = SSD_HEADS // SSD_GROUPS
    pairs_per_group = heads_per_group // 2
    ys = []
    for g in range(SSD_GROUPS):
        bg = bm[:, g * SSD_STATE:(g + 1) * SSD_STATE]
        cg = cm[:, g * SSD_STATE:(g + 1) * SSD_STATE]
        cb = lax.dot_general(cg.astype(BF16), bg.astype(BF16), NT_DIMS, preferred_element_type=F32)
        bg_t = bg.T
        for pp in range(pairs_per_group):
            pair = g * pairs_per_group + pp
            x_pair = xs[:, pair * LANES:(pair + 1) * LANES].astype(BF16)
            st = state_ref[pair]
            rhs = jnp.concatenate([x_pair, st.astype(BF16)], axis=0)
            y_h, s_h, e_h = [], [], []
            for hl in range(2):
                h = g * heads_per_group + 2 * pp + hl
                seg = a_cum[:, h:h + 1] - a_cum_t[h:h + 1, :]
                decay = jnp.exp(jnp.where(tri, seg, NEG))
                m_h = (cb * decay * dt_t[h:h + 1, :]).astype(BF16)
                c_h = (cg * ea[:, h:h + 1]).astype(BF16)
                lhs = jnp.concatenate([m_h, c_h], axis=1)
                y_h.append(jnp.dot(lhs, rhs, preferred_element_type=F32))
                b_s = (bg_t * dtw_t[h:h + 1, :]).astype(BF16)
                s_h.append(jnp.dot(b_s, x_pair, preferred_element_type=F32))
                e_h.append(e_last[:, h:h + 1])
            ys.append(jnp.where(first_half, y_h[0], y_h[1]))
            state_ref[pair] = (st * jnp.where(first_half, e_h[0], e_h[1])
                               + jnp.where(first_half, s_h[0], s_h[1]))
    y = jnp.concatenate(ys, axis=1)
    y = y + dskip_ref[...] * xs
    y = y * _silu(z_ref[0].astype(F32))
    gw = D_INNER // SSD_GROUPS
    outs = []
    for g in range(SSD_GROUPS):
        yg = y[:, g * gw:(g + 1) * gw]
        ms = jnp.mean(yg * yg, axis=-1, keepdims=True)
        outs.append(yg * lax.rsqrt(ms + LN_EPS) * normw_ref[:, g * gw:(g + 1) * gw])
    y_ref[0] = jnp.concatenate(outs, axis=1).astype(y_ref.dtype)


def _ssd(proj, dt_raw, xbc_blk, z_blk, conv_w, conv_b, dt_bias, a_log, d_skip, norm_w, L):
    b, s, _ = proj.shape
    const = lambda shape: pl.BlockSpec(shape, lambda bi, c: (0,) * len(shape))
    return pl.pallas_call(
        functools.partial(_ssd_kernel, L=L),
        grid=(b, s // L),
        in_specs=[pl.BlockSpec((1, L, CONV_DIM), lambda bi, c: (bi, c, xbc_blk)),
                  pl.BlockSpec((1, L, LANES), lambda bi, c: (bi, c, 0)),
                  pl.BlockSpec((1, L, D_INNER), lambda bi, c: (bi, c, z_blk)),
                  const((CONV_W, CONV_DIM)), const((1, CONV_DIM)), const((1, LANES)), const((1, LANES)),
                  const((1, D_INNER)), const((1, D_INNER))],
        out_specs=pl.BlockSpec((1, L, D_INNER), lambda bi, c: (bi, c, 0)),
        out_shape=jax.ShapeDtypeStruct((b, s, D_INNER), BF16),
        scratch_shapes=[pltpu.VMEM((L + 2 * SUBLANES, CONV_DIM), F32),
                        pltpu.VMEM((SSD_HEADS // 2, SSD_STATE, LANES), F32)],
        compiler_params=pltpu.CompilerParams(dimension_semantics=("parallel", "arbitrary"),
                                             vmem_limit_bytes=VMEM_LIMIT),
        name="ssd",
    )(proj, dt_raw, proj, conv_w, conv_b, dt_bias, a_log, d_skip, norm_w)


def _merge_kernel(attn_ref, y_ref, ga_ref, gs_ref, x_ref, wa_ref, ws_ref, wo_ref, g_ref, b_ref, o_ref, op_ref,
                  *, tm):
    a = jnp.dot(attn_ref[...], wa_ref[...], preferred_element_type=F32)
    s = jnp.dot(y_ref[...], ws_ref[...], preferred_element_type=F32)
    merged = _sigmoid(ga_ref[...].astype(F32)) * a + _sigmoid(gs_ref[...].astype(F32)) * s
    out = jnp.dot(merged.astype(BF16), wo_ref[...], preferred_element_type=F32)
    h = _layer_norm(DN_ALPHA * x_ref[...] + out, g_ref[...], b_ref[...])
    _store_token_tiles(o_ref, h, tm)
    _store_packed_tokens(op_ref, h, tm)


def _merge(attn2d, y2d, proj2d, gate_blk, x2d, wa, ws, wo, g, b, tm):
    t = x2d.shape[0]
    const = lambda shape: pl.BlockSpec(shape, lambda i: (0,) * len(shape))
    return pl.pallas_call(
        functools.partial(_merge_kernel, tm=tm),
        grid=(t // tm,),
        in_specs=[pl.BlockSpec((tm, DA_WIDTH), lambda i: (i, 0)),
                  pl.BlockSpec((tm, D_INNER), lambda i: (i, 0)),
                  pl.BlockSpec((tm, D_MODEL), lambda i: (i, gate_blk)),
                  pl.BlockSpec((tm, D_MODEL), lambda i: (i, gate_blk + 1)),
                  pl.BlockSpec((tm, D_MODEL), lambda i: (i, 0)),
                  const((DA_WIDTH, D_MODEL)), const((D_INNER, D_MODEL)), const((D_MODEL, D_MODEL)),
                  const((1, D_MODEL)), const((1, D_MODEL))],
        out_specs=[pl.BlockSpec((tm * ROW_TILES, LANES), lambda i: (i, 0)),
                   pl.BlockSpec((tm * PACK_ROWS, LANES), lambda i: (i, 0))],
        out_shape=[jax.ShapeDtypeStruct((t * ROW_TILES, LANES), F32),
                   jax.ShapeDtypeStruct((t * PACK_ROWS, LANES), U32)],
        compiler_params=pltpu.CompilerParams(dimension_semantics=("parallel",), vmem_limit_bytes=VMEM_LIMIT),
        name="merge_ln1",
    )(attn2d, y2d, proj2d, proj2d, x2d, wa, ws, wo, g, b)


def _router_kernel(h_ref, w_ref, bias_ref, idx_ref, wgt_ref, rank_ref, cnt_ref, carry_ref, *, tt):
    i = pl.program_id(0)

    @pl.when(i == 0)
    def _():
        carry_ref[...] = jnp.zeros(carry_ref.shape, F32)

    h = _load_token_tiles(h_ref, tt)
    h_hi = h.astype(BF16)
    h_lo = (h - h_hi.astype(F32)).astype(BF16)
    w = w_ref[...]
    w_hi = w.astype(BF16)
    w_lo = (w - w_hi.astype(F32)).astype(BF16)
    logits = (lax.dot_general(w_hi, h_hi, NT_DIMS, preferred_element_type=F32)
              + lax.dot_general(w_hi, h_lo, NT_DIMS, preferred_element_type=F32)
              + lax.dot_general(w_lo, h_hi, NT_DIMS, preferred_element_type=F32))
    scores = _sigmoid(logits)
    choice = scores + bias_ref[...]

    gsz = N_EXPERTS // N_EXPERT_GROUPS
    iota_g = lax.broadcasted_iota(I32, (gsz, tt), 0)
    rows = []
    for g in range(N_EXPERT_GROUPS):
        cg = choice[g * gsz:(g + 1) * gsz, :]
        m1 = jnp.max(cg, axis=0, keepdims=True)
        i1 = jnp.min(jnp.where(cg == m1, iota_g, gsz), axis=0, keepdims=True)
        m2 = jnp.max(jnp.where(iota_g == i1, NEG, cg), axis=0, keepdims=True)
        rows.append(m1 + m2)
    gscore = jnp.concatenate(rows, axis=0)

    iota_grp = lax.broadcasted_iota(I32, (N_EXPERT_GROUPS, tt), 0)
    sel = jnp.zeros((N_EXPERT_GROUPS, tt), F32)
    cur = gscore
    for _ in range(TOPK_GROUPS):
        m = jnp.max(cur, axis=0, keepdims=True)
        ig = jnp.min(jnp.where(cur == m, iota_grp, N_EXPERT_GROUPS), axis=0, keepdims=True)
        hit = iota_grp == ig
        sel = jnp.where(hit, 1.0, sel)
        cur = jnp.where(hit, NEG, cur)
    sel_e = jnp.concatenate([jnp.broadcast_to(sel[g:g + 1, :], (gsz, tt)) for g in range(N_EXPERT_GROUPS)], axis=0)
    cur = jnp.where(sel_e > 0.5, choice, NEG)

    iota_e = lax.broadcasted_iota(I32, (N_EXPERTS, tt), 0)
    idx_rows, w_rows, hits = [], [], []
    for _ in range(TOP_K):
        m = jnp.max(cur, axis=0, keepdims=True)
        ik = jnp.min(jnp.where(cur == m, iota_e, N_EXPERTS), axis=0, keepdims=True)
        hit = iota_e == ik
        w_rows.append(jnp.sum(jnp.where(hit, scores, 0.0), axis=0, keepdims=True))
        cur = jnp.where(hit, NEG, cur)
        idx_rows.append(ik)
        hits.append(hit)
    wv = jnp.concatenate(w_rows, axis=0)
    wv = wv / jnp.sum(wv, axis=0, keepdims=True) * ROUTED_SCALE
    idx_ref[...] = jnp.concatenate(idx_rows, axis=0)
    wgt_ref[...] = wv

    onehot = jnp.zeros((N_EXPERTS, tt), F32)
    for hit in hits:
        onehot = jnp.where(hit, 1.0, onehot)
    onehot_b = onehot.astype(BF16)
    ti = lax.broadcasted_iota(I32, (tt, tt), 0)
    tj = lax.broadcasted_iota(I32, (tt, tt), 1)
    before = jnp.where(ti < tj, 1.0, 0.0).astype(BF16)
    prefix = jnp.dot(onehot_b, before, preferred_element_type=F32) + carry_ref[...]
    rank_rows = [jnp.sum(jnp.where(hit, prefix, 0.0), axis=0, keepdims=True) for hit in hits]
    rank_ref[...] = jnp.concatenate(rank_rows, axis=0).astype(I32)
    total = jnp.dot(onehot_b, jnp.ones((tt, tt), BF16), preferred_element_type=F32)
    new_carry = carry_ref[...] + total
    carry_ref[...] = new_carry
    cnt_ref[...] = new_carry[:, 0:LANES]


def _router(h_tiles, w_router_t, bias_col, tt):
    t = h_tiles.shape[0] // ROW_TILES
    kt = pl.BlockSpec((TOP_K, tt), lambda i: (0, i))
    return pl.pallas_call(
        functools.partial(_router_kernel, tt=tt),
        grid=(t // tt,),
        in_specs=[pl.BlockSpec((tt * ROW_TILES, LANES), lambda i: (i, 0)),
                  pl.BlockSpec((N_EXPERTS, D_MODEL), lambda i: (0, 0)),
                  pl.BlockSpec((N_EXPERTS, 1), lambda i: (0, 0))],
        out_specs=[kt, kt, kt, pl.BlockSpec((N_EXPERTS, LANES), lambda i: (0, 0))],
        out_shape=[jax.ShapeDtypeStruct((TOP_K, t), I32), jax.ShapeDtypeStruct((TOP_K, t), F32),
                   jax.ShapeDtypeStruct((TOP_K, t), I32), jax.ShapeDtypeStruct((N_EXPERTS, LANES), F32)],
        scratch_shapes=[pltpu.VMEM((N_EXPERTS, tt), F32)],
        compiler_params=pltpu.CompilerParams(dimension_semantics=("arbitrary",), vmem_limit_bytes=VMEM_LIMIT),
        name="moe_router",
    )(h_tiles, w_router_t, bias_col)


def _pos_kernel(idx_ref, rank_ref, base_ref, pos_ref, *, tt):
    iota_e = lax.broadcasted_iota(I32, (N_EXPERTS, tt), 0)
    base = base_ref[...]
    rows = []
    for k in range(TOP_K):
        hit = iota_e == idx_ref[k:k + 1, :]
        rows.append(jnp.sum(jnp.where(hit, base, 0.0), axis=0, keepdims=True))
    pos_ref[...] = jnp.concatenate(rows, axis=0).astype(I32) + rank_ref[...]


def _positions(idx_t, rank_t, base_col, tt):
    t = idx_t.shape[1]
    kt = pl.BlockSpec((TOP_K, tt), lambda i: (0, i))
    return pl.pallas_call(
        functools.partial(_pos_kernel, tt=tt),
        grid=(t // tt,),
        in_specs=[kt, kt, pl.BlockSpec((N_EXPERTS, 1), lambda i: (0, 0))],
        out_specs=kt,
        out_shape=jax.ShapeDtypeStruct((TOP_K, t), I32),
        compiler_params=pltpu.CompilerParams(dimension_semantics=("parallel",)),
        name="moe_positions",
    )(idx_t, rank_t, base_col)


def _dispatch_kernel(pos_ref, h_ref, xs_ref, sem, *, tt):
    def issue(t, carry):
        for k in range(TOP_K):
            pltpu.make_async_copy(h_ref.at[_packed_rows(t * PACK_ROWS), :],
                                  xs_ref.at[_packed_rows(pos_ref[k, t] * PACK_ROWS), :], sem).start(priority=k % 2)
        return carry

    lax.fori_loop(0, tt, issue, 0)
    done = xs_ref.at[pl.ds(0, tt * TOP_K * PACK_ROWS), :]
    pltpu.make_async_copy(done, done, sem).wait()


def _dispatch(pos_t, h_packed, n_rows, tt):
    t = pos_t.shape[1]
    return pl.pallas_call(
        functools.partial(_dispatch_kernel, tt=tt),
        grid=(t // tt,),
        in_specs=[pl.BlockSpec((TOP_K, tt), lambda i: (0, i), memory_space=pltpu.SMEM),
                  pl.BlockSpec((tt * PACK_ROWS, LANES), lambda i: (i, 0))],
        out_specs=pl.BlockSpec(memory_space=pl.ANY),
        out_shape=jax.ShapeDtypeStruct((n_rows * PACK_ROWS, LANES), U32),
        scratch_shapes=[pltpu.SemaphoreType.DMA(())],
        compiler_params=pltpu.CompilerParams(dimension_semantics=("arbitrary",), has_side_effects=True),
        name="moe_dispatch",
    )(pos_t, h_packed)


def _expert_kernel(be_ref, nv_ref, nu_ref, xs_ref, wg_ref, wu_ref, wd_ref, ys_ref, wgb_ref, wub_ref, wdb_ref,
                   *, blk):
    j = pl.program_id(0)

    @pl.when(j < nu_ref[0])
    def _():
        @pl.when((j == 0) | (be_ref[j] != be_ref[jnp.maximum(j - 1, 0)]))
        def _():
            wgb_ref[...] = wg_ref[0].astype(BF16)
            wub_ref[...] = wu_ref[0].astype(BF16)
            wdb_ref[...] = wd_ref[0].astype(BF16)

        x = _load_packed_tokens(xs_ref, blk).astype(BF16)
        g = jnp.dot(x, wgb_ref[...], preferred_element_type=F32)
        u = jnp.dot(x, wub_ref[...], preferred_element_type=F32)
        valid = lax.broadcasted_iota(I32, (blk, 1), 0) < nv_ref[j]
        hmid = jnp.where(valid, _silu(g) * u, 0.0).astype(BF16)
        _store_packed_tokens(ys_ref, jnp.dot(hmid, wdb_ref[...], preferred_element_type=F32), blk)


def _experts(blk_expert, blk_valid, n_used, xs, wg, wu, wd, blk):
    n_blocks = blk_expert.shape[0]
    rows = lambda j, be, nv, nu: (jnp.minimum(j, nu[0] - 1), 0)
    wsel = lambda j, be, nv, nu: (be[j], 0, 0)
    return pl.pallas_call(
        functools.partial(_expert_kernel, blk=blk),
        grid_spec=pltpu.PrefetchScalarGridSpec(
            num_scalar_prefetch=3,
            grid=(n_blocks,),
            in_specs=[pl.BlockSpec((blk * PACK_ROWS, LANES), rows),
                      pl.BlockSpec((1, D_MODEL, D_EXPERT), wsel),
                      pl.BlockSpec((1, D_MODEL, D_EXPERT), wsel),
                      pl.BlockSpec((1, D_EXPERT, D_MODEL), wsel)],
            out_specs=pl.BlockSpec((blk * PACK_ROWS, LANES), rows),
            scratch_shapes=[pltpu.VMEM((D_MODEL, D_EXPERT), BF16), pltpu.VMEM((D_MODEL, D_EXPERT), BF16),
                            pltpu.VMEM((D_EXPERT, D_MODEL), BF16)]),
        out_shape=jax.ShapeDtypeStruct(xs.shape, U32),
        compiler_params=pltpu.CompilerParams(dimension_semantics=("arbitrary",), vmem_limit_bytes=VMEM_LIMIT),
        name="moe_experts",
    )(blk_expert, blk_valid, n_used, xs, wg, wu, wd)


def _combine_kernel(pos_ref, wgt_ref, h_ref, wsg_ref, wsu_ref, wsd_ref, g_ref, b_ref, ys_ref, o_ref,
                    buf_ref, sem, *, tc):
    def issue(t, carry):
        for k in range(TOP_K):
            pltpu.make_async_copy(ys_ref.at[_packed_rows(pos_ref[k, t] * PACK_ROWS), :],
                                  buf_ref.at[_packed_rows((k * tc + t) * PACK_ROWS), :], sem).start(priority=k % 2)
        return carry

    lax.fori_loop(0, tc, issue, 0)
    h = _load_token_tiles(h_ref, tc)
    hb = h.astype(BF16)
    g = jnp.dot(hb, wsg_ref[...], preferred_element_type=F32)
    u = jnp.dot(hb, wsu_ref[...], preferred_element_type=F32)
    shared = jnp.dot((_silu(g) * u).astype(BF16), wsd_ref[...], preferred_element_type=F32)
    pltpu.make_async_copy(buf_ref, buf_ref, sem).wait()
    wgt = wgt_ref[...]
    routed = None
    for k in range(TOP_K):
        term = wgt[:, k:k + 1] * _load_packed_tokens(buf_ref, tc, k * tc)
        routed = term if routed is None else routed + term
    o_ref[...] = _layer_norm(DN_ALPHA * h + (routed + shared), g_ref[...], b_ref[...])


def _combine(pos_t, wgt, h_tiles, wsg, wsu, wsd, g, b, ys, tc):
    t = pos_t.shape[1]
    const = lambda shape: pl.BlockSpec(shape, lambda i: (0,) * len(shape))
    return pl.pallas_call(
        functools.partial(_combine_kernel, tc=tc),
        grid=(t // tc,),
        in_specs=[pl.BlockSpec((TOP_K, tc), lambda i: (0, i), memory_space=pltpu.SMEM),
                  pl.BlockSpec((tc, TOP_K), lambda i: (i, 0)),
                  pl.BlockSpec((tc * ROW_TILES, LANES), lambda i: (i, 0)),
                  const((D_MODEL, D_EXPERT)), const((D_MODEL, D_EXPERT)), const((D_EXPERT, D_MODEL)),
                  const((1, D_MODEL)), const((1, D_MODEL)),
                  pl.BlockSpec(memory_space=pl.ANY)],
        out_specs=pl.BlockSpec((tc, D_MODEL), lambda i: (i, 0)),
        out_shape=jax.ShapeDtypeStruct((t, D_MODEL), F32),
        scratch_shapes=[pltpu.VMEM((TOP_K * tc * PACK_ROWS, LANES), U32), pltpu.SemaphoreType.DMA(())],
        compiler_params=pltpu.CompilerParams(dimension_semantics=("arbitrary",), vmem_limit_bytes=VMEM_LIMIT),
        name="moe_combine_ln2",
    )(pos_t, wgt, h_tiles, wsg, wsu, wsd, g, b, ys)


def _pick(n, pref):
    t = min(n, pref)
    assert n % t == 0, (n, t)
    return t


def _layer(x, w_in, lq1, lk1, lq2, lk2, subw, conv_w, conv_b, dt_bias, a_log, d_skip, ssd_norm_w,
           w_br_attn, w_br_ssd, w_out, ln1_g, ln1_b, w_router, router_bias, w_eg, w_eu, w_ed,
           w_sg, w_su, w_sd, ln2_g, ln2_b, lam_init):
    b, s, d = x.shape
    t = b * s
    assert d == D_MODEL and s % SSD_CHUNK == 0
    x2d = x.reshape(t, d)

    o_q, o_z, o_xbc = 0, 3 * DA_WIDTH, 3 * DA_WIDTH + D_INNER
    o_dt = o_xbc + CONV_DIM
    o_g = o_dt + SSD_HEADS
    w_all = jnp.concatenate([w_in[:, o_xbc:o_dt], w_in[:, o_q:o_z], w_in[:, o_z:o_xbc],
                             w_in[:, o_g:o_g + 2 * D_MODEL]], axis=1).astype(BF16)
    w_dt = jnp.pad(w_in[:, o_dt:o_g], ((0, 0), (0, LANES - SSD_HEADS))).astype(BF16)
    xbc_blk, q_col0 = 0, CONV_DIM // LANES
    z_blk = (CONV_DIM + 3 * DA_WIDTH) // D_INNER
    gate_blk = (CONV_DIM + 3 * DA_WIDTH + D_INNER) // D_MODEL
    n_proj = w_all.shape[1]
    proj, dt_raw = _proj(x2d, w_all, w_dt, _pick(t, 2048), 1024)
    proj3 = proj.reshape(b, s, n_proj)

    attn = _attention(proj3, q_col0, lq1[None], lk1[None], lq2[None], lk2[None], subw[None],
                      lam_init, _pick(s, 512))

    pad = LANES - SSD_HEADS
    y = _ssd(proj3, dt_raw.reshape(b, s, LANES), xbc_blk, z_blk, conv_w, conv_b[None],
             jnp.pad(dt_bias, (0, pad))[None], jnp.pad(a_log, (0, pad))[None],
             jnp.repeat(d_skip, SSD_HEAD_DIM)[None], ssd_norm_w[None], SSD_CHUNK)

    h1, h1p = _merge(attn.reshape(t, DA_WIDTH), y.reshape(t, D_INNER), proj, gate_blk, x2d, w_br_attn.astype(BF16),
                w_br_ssd.astype(BF16), w_out.astype(BF16), ln1_g[None], ln1_b[None], _pick(t, 512))

    tt = _pick(t, 512)
    idx_t, wgt_t, rank_t, cnt = _router(h1, w_router.T, router_bias[:, None], tt)
    blk = MOE_ROW_BLOCK
    n_blocks = (t * TOP_K) // blk + N_EXPERTS
    counts = cnt[:, 0].astype(I32)
    blocks_per_e = (counts + blk - 1) // blk
    blk_end = jnp.cumsum(blocks_per_e)
    blk_start = blk_end - blocks_per_e
    n_used = blk_end[-1:]
    jblk = jnp.minimum(jnp.arange(n_blocks, dtype=I32), n_used[0] - 1)
    blk_expert = jnp.minimum(jnp.sum((blk_end[None, :] <= jblk[:, None]).astype(I32), axis=1), N_EXPERTS - 1)
    blk_valid = jnp.clip(counts[blk_expert] - (jblk - blk_start[blk_expert]) * blk, 0, blk).astype(I32)
    base_col = (blk_start * blk).astype(F32)[:, None]
    pos_t = _positions(idx_t, rank_t, base_col, tt)

    xs = _dispatch(pos_t, h1p, n_blocks * blk, tt)
    ys = _experts(blk_expert, blk_valid, n_used.astype(I32), xs, w_eg, w_eu, w_ed, blk)
    out = _combine(pos_t, wgt_t.T, h1, w_sg.astype(BF16), w_su.astype(BF16), w_sd.astype(BF16),
                   ln2_g[None], ln2_b[None], ys, _pick(t, 256))
    return out.reshape(b, s, d)


def kernel(x, w_in, lambda_q1, lambda_k1, lambda_q2, lambda_k2, attn_subln_w, conv_w, conv_b, dt_bias, a_log,
           d_skip, ssd_norm_w, w_br_attn, w_br_ssd, w_out, ln1_g, ln1_b, w_router, router_bias, w_exp_gate,
           w_exp_up, w_exp_down, w_sh_gate, w_sh_up, w_sh_down, ln2_g, ln2_b):
    h = x
    for layer in range(DEPTH):
        lam_init = 0.8 - 0.6 * math.exp(-0.3 * layer)
        h = _layer(h, w_in[layer], lambda_q1[layer], lambda_k1[layer], lambda_q2[layer], lambda_k2[layer],
                   attn_subln_w[layer], conv_w[layer], conv_b[layer], dt_bias[layer], a_log[layer],
                   d_skip[layer], ssd_norm_w[layer], w_br_attn[layer], w_br_ssd[layer], w_out[layer],
                   ln1_g[layer], ln1_b[layer], w_router[layer], router_bias[layer], w_exp_gate[layer],
                   w_exp_up[layer], w_exp_down[layer], w_sh_gate[layer], w_sh_up[layer], w_sh_down[layer],
                   ln2_g[layer], ln2_b[layer], lam_init)
    return h
```

```python
import functools
import math

import jax
import jax.numpy as jnp
from jax import lax
from jax.experimental import pallas as pl
from jax.experimental.pallas import tpu as pltpu

F32 = jnp.float32
BF16 = jnp.bfloat16
I32 = jnp.int32

D_MODEL = 1024
DEPTH = 1
DA_HEADS = 8
DA_HEAD_DIM = 64
DA_WIDTH = DA_HEADS * 2 * DA_HEAD_DIM
D_INNER = 2048
SSD_HEAD_DIM = 64
SSD_HEADS = D_INNER // SSD_HEAD_DIM
SSD_GROUPS = 4
SSD_STATE = 128
CONV_W = 4
CONV_DIM = D_INNER + 2 * SSD_GROUPS * SSD_STATE
SSD_CHUNK = 128
N_EXPERTS = 256
TOP_K = 8
N_EXPERT_GROUPS = 8
TOPK_GROUPS = 4
D_EXPERT = 256
ROUTED_SCALE = 2.5
DN_ALPHA = (2.0 * DEPTH) ** 0.25
LN_EPS = 1e-5

LANES = 128
SUBLANES = 8
ROW_TILES = D_MODEL // LANES
NEG = -1e30
MOE_ROW_BLOCK = 1024
VMEM_LIMIT = 56 * 1024 * 1024

NT_DIMS = (((1,), (1,)), ((), ()))


def _sigmoid(v):
    return 0.5 * jnp.tanh(0.5 * v) + 0.5


def _silu(v):
    hv = 0.5 * v
    return hv * jnp.tanh(hv) + hv


def _layer_norm(v, g, b):
    mu = jnp.mean(v, axis=-1, keepdims=True)
    d = v - mu
    var = jnp.mean(d * d, axis=-1, keepdims=True)
    return d * lax.rsqrt(var + LN_EPS) * g + b


def _load_token_tiles(ref, n_tok, row0=0):
    return jnp.concatenate(
        [ref[pl.ds(row0 + s, n_tok, stride=ROW_TILES), :] for s in range(ROW_TILES)], axis=-1)


def _store_token_tiles(ref, val, n_tok):
    for s in range(ROW_TILES):
        ref[pl.ds(s, n_tok, stride=ROW_TILES), :] = val[:, s * LANES:(s + 1) * LANES]


PACK_ROWS = ROW_TILES // 2
U32 = jnp.uint32


def _packed_rows(row0):
    return pl.ds(pl.multiple_of(row0, PACK_ROWS), PACK_ROWS)


def _store_packed_tokens(ref, val, n_tok):
    for q in range(PACK_ROWS):
        lo = pltpu.bitcast(val[:, (2 * q) * LANES:(2 * q + 1) * LANES].astype(BF16).astype(F32), U32)
        hi = pltpu.bitcast(val[:, (2 * q + 1) * LANES:(2 * q + 2) * LANES].astype(BF16).astype(F32), U32)
        ref[pl.ds(q, n_tok, stride=PACK_ROWS), :] = (lo >> 16) | hi


def _load_packed_tokens(ref, n_tok, tok0=0):
    chunks = []
    for q in range(PACK_ROWS):
        w = ref[pl.ds(tok0 * PACK_ROWS + q, n_tok, stride=PACK_ROWS), :]
        chunks.append(pltpu.bitcast(w << 16, F32))
        chunks.append(pltpu.bitcast(w & jnp.uint32(0xFFFF0000), F32))
    return jnp.concatenate(chunks, axis=-1)


def _split3(v):
    hi = v.astype(BF16)
    r1 = v - hi.astype(F32)
    mid = r1.astype(BF16)
    lo = (r1 - mid.astype(F32)).astype(BF16)
    return hi, mid, lo


def _proj_kernel(x_ref, w_ref, wdt_ref, o_ref, dt_ref, xb_ref):
    @pl.when(pl.program_id(1) == 0)
    def _():
        xb = x_ref[...].astype(BF16)
        xb_ref[...] = xb
        dt_ref[...] = jnp.dot(xb, wdt_ref[...], preferred_element_type=F32)

    o_ref[...] = jnp.dot(xb_ref[...], w_ref[...], preferred_element_type=F32).astype(o_ref.dtype)


def _proj(x2d, w, w_dt, tm, tn):
    t, k = x2d.shape
    n = w.shape[1]
    return pl.pallas_call(
        _proj_kernel,
        grid=(t // tm, n // tn),
        in_specs=[pl.BlockSpec((tm, k), lambda i, j: (i, 0)),
                  pl.BlockSpec((k, tn), lambda i, j: (0, j)),
                  pl.BlockSpec((k, LANES), lambda i, j: (0, 0))],
        out_specs=[pl.BlockSpec((tm, tn), lambda i, j: (i, j)),
                   pl.BlockSpec((tm, LANES), lambda i, j: (i, 0))],
        out_shape=[jax.ShapeDtypeStruct((t, n), BF16), jax.ShapeDtypeStruct((t, LANES), F32)],
        scratch_shapes=[pltpu.VMEM((tm, k), BF16)],
        compiler_params=pltpu.CompilerParams(dimension_semantics=("parallel", "arbitrary"),
                                             vmem_limit_bytes=VMEM_LIMIT),
        name="proj",
    )(x2d, w, w_dt)


ATTN_ROW_CHUNK = 64


def _attn_kernel(lq1_ref, lk1_ref, lq2_ref, lk2_ref, subw_ref, q_ref, k_ref, v_ref, o_ref,
                 qs_ref, m_ref, l_ref, acc_ref, *, tq, lam_init):
    i = pl.program_id(2)
    rows = 2 * tq
    ch = ATTN_ROW_CHUNK
    q = q_ref[0].astype(F32) * (DA_HEAD_DIM ** -0.5 * math.log2(math.e))
    lane = lax.broadcasted_iota(I32, (tq, LANES), 1)
    qs_ref[0:tq, :] = jnp.where(lane < DA_HEAD_DIM, q, 0.0).astype(BF16)
    qs_ref[tq:rows, :] = jnp.where(lane >= DA_HEAD_DIM, q, 0.0).astype(BF16)
    m_ref[...] = jnp.full((rows, LANES), NEG, F32)
    l_ref[...] = jnp.zeros((rows, LANES), F32)
    acc_ref[...] = jnp.zeros((rows, LANES), F32)
    qs = qs_ref[...]

    def step(j, diagonal):
        start = pl.multiple_of(j * tq, tq)
        kb = k_ref[0, pl.ds(start, tq), :]
        vb = v_ref[0, pl.ds(start, tq), :]
        s = lax.dot_general(qs, kb, NT_DIMS, preferred_element_type=F32)
        ps, alphas = [], []
        for c in range(rows // ch):
            r0 = c * ch
            sc = s[r0:r0 + ch, :]
            if diagonal:
                row = (r0 % tq) + lax.broadcasted_iota(I32, (ch, tq), 0)
                col = lax.broadcasted_iota(I32, (ch, tq), 1)
                sc = jnp.where(col <= row, sc, NEG)
            m_prev = m_ref[r0:r0 + ch, :]
            m_new = jnp.maximum(m_prev, jnp.max(sc, axis=-1, keepdims=True))
            alpha = jnp.exp2(m_prev - m_new)
            p = jnp.exp2(sc - jnp.concatenate([m_new] * (tq // LANES), axis=1))
            psum = p[:, 0:LANES]
            for t in range(1, tq // LANES):
                psum = psum + p[:, t * LANES:(t + 1) * LANES]
            l_ref[r0:r0 + ch, :] = alpha * l_ref[r0:r0 + ch, :] + psum
            m_ref[r0:r0 + ch, :] = m_new
            ps.append(p.astype(BF16))
            alphas.append(alpha)
        pv = jnp.dot(jnp.concatenate(ps, axis=0), vb, preferred_element_type=F32)
        acc_ref[...] = jnp.concatenate(alphas, axis=0) * acc_ref[...] + pv

    def full_block_pair(jj, carry):
        step(2 * jj, False)
        step(2 * jj + 1, False)
        return carry

    lax.fori_loop(0, i // 2, full_block_pair, 0)

    @pl.when(i % 2 == 1)
    def _():
        step(i - 1, False)

    step(i, True)
    o = acc_ref[...] / jnp.sum(l_ref[...], axis=-1, keepdims=True)
    lam = (jnp.exp(jnp.sum(lq1_ref[...] * lk1_ref[...], axis=-1, keepdims=True))
           - jnp.exp(jnp.sum(lq2_ref[...] * lk2_ref[...], axis=-1, keepdims=True)) + lam_init)
    d = o[0:tq, :] - lam * o[tq:rows, :]
    ms = jnp.mean(d * d, axis=-1, keepdims=True)
    o_ref[0] = (d * lax.rsqrt(ms + LN_EPS) * subw_ref[...] * (1.0 - lam_init)).astype(o_ref.dtype)


def _attention(proj, col0, lq1, lk1, lq2, lk2, subw, lam_init, tq):
    b, s, _ = proj.shape
    vec = pl.BlockSpec((1, DA_HEAD_DIM), lambda bi, h, i: (0, 0))
    return pl.pallas_call(
        functools.partial(_attn_kernel, tq=tq, lam_init=lam_init),
        grid=(b, DA_HEADS, s // tq),
        in_specs=[vec, vec, vec, vec,
                  pl.BlockSpec((1, LANES), lambda bi, h, i: (0, 0)),
                  pl.BlockSpec((1, tq, LANES), lambda bi, h, i: (bi, i, col0 + h)),
                  pl.BlockSpec((1, s, LANES), lambda bi, h, i: (bi, 0, col0 + DA_HEADS + h)),
                  pl.BlockSpec((1, s, LANES), lambda bi, h, i: (bi, 0, col0 + 2 * DA_HEADS + h))],
        out_specs=pl.BlockSpec((1, tq, LANES), lambda bi, h, i: (bi, i, h)),
        out_shape=jax.ShapeDtypeStruct((b, s, DA_WIDTH), BF16),
        scratch_shapes=[pltpu.VMEM((2 * tq, LANES), BF16), pltpu.VMEM((2 * tq, LANES), F32),
                        pltpu.VMEM((2 * tq, LANES), F32), pltpu.VMEM((2 * tq, LANES), F32)],
        compiler_params=pltpu.CompilerParams(dimension_semantics=("parallel", "parallel", "arbitrary"),
                                             vmem_limit_bytes=VMEM_LIMIT),
        name="diff_attn",
    )(lq1, lk1, lq2, lk2, subw, proj, proj, proj)


CONV_HALO = 16


def _ssd_kernel(xbc_ref, dt_ref, z_ref, shift_ref, convw_ref, convb_ref, dtb_ref, alog_ref, dskip_ref, normw_ref,
                y_ref, ext_ref, state_ref, *, L):
    c = pl.program_id(1)
    halo = CONV_HALO

    @pl.when(c == 0)
    def _():
        ext_ref[0:halo, :] = jnp.zeros((halo, CONV_DIM), BF16)
        state_ref[...] = jnp.zeros(state_ref.shape, F32)

    ext_ref[halo:halo + L, :] = xbc_ref[0]
    delayed = jnp.dot(shift_ref[...], ext_ref[...], preferred_element_type=F32)
    acc = convb_ref[...] + convw_ref[CONV_W - 1:CONV_W, :] * xbc_ref[0].astype(F32)
    for d in range(1, CONV_W):
        acc = acc + convw_ref[CONV_W - 1 - d:CONV_W - d, :] * delayed[(d - 1) * L:d * L, :]
    ext_ref[0:halo, :] = ext_ref[L:L + halo, :]
    xbc = _silu(acc)
    xs = xbc[:, 0:D_INNER]
    bm = xbc[:, D_INNER:D_INNER + SSD_GROUPS * SSD_STATE]
    cm = xbc[:, D_INNER + SSD_GROUPS * SSD_STATE:CONV_DIM]

    dt_in = dt_ref[0] + dtb_ref[...]
    dt = jnp.maximum(dt_in, 0.0) + jnp.log1p(jnp.exp(-jnp.abs(dt_in)))
    a = -jnp.exp(alog_ref[...]) * math.log2(math.e)
    da = dt * a
    row = lax.broadcasted_iota(I32, (L, L), 0)
    col = lax.broadcasted_iota(I32, (L, L), 1)
    tri = row >= col
    tril = jnp.where(tri, 1.0, 0.0).astype(BF16)
    hi, mid, lo = _split3(da)
    a_cum = (jnp.dot(tril, hi, preferred_element_type=F32) + jnp.dot(tril, mid, preferred_element_type=F32)
             + jnp.dot(tril, lo, preferred_element_type=F32))
    a_last = a_cum[L - 1:L, :]
    a_cum_t = a_cum.T
    dt_t = dt.T
    dtw_t = (dt * jnp.exp2(a_last - a_cum)).T
    ea = jnp.exp2(a_cum)
    e_last = jnp.exp2(a_last)
    lane = lax.broadcasted_iota(I32, (1, LANES), 1)
    first_half = lane < SSD_HEAD_DIM

    heads_per_group = SSD_HEADS // SSD_GROUPS
    pairs_per_group = heads_per_group // 2
    ys = []
    for g in range(SSD_GROUPS):
        bg = bm[:, g * SSD_STATE:(g + 1) * SSD_STATE]
        cg = cm[:, g * SSD_STATE:(g + 1) * SSD_STATE]
        cb = lax.dot_general(cg.astype(BF16), bg.astype(BF16), NT_DIMS, preferred_element_type=F32)
        bg_t = bg.T
        for pp in range(pairs_per_group):
            pair = g * pairs_per_group + pp
            x_pair = xs[:, pair * LANES:(pair + 1) * LANES].astype(BF16)
            st = state_ref[pair]
            rhs = jnp.concatenate([x_pair, st.astype(BF16)], axis=0)
            y_h, s_h, e_h = [], [], []
            for hl in range(2):
                h = g * heads_per_group + 2 * pp + hl
                seg = a_cum[:, h:h + 1] - a_cum_t[h:h + 1, :]
                decay = jnp.exp2(jnp.where(tri, seg, NEG))
                m_h = (cb * decay * dt_t[h:h + 1, :]).astype(BF16)
                c_h = (cg * ea[:, h:h + 1]).astype(BF16)
                lhs = jnp.concatenate([m_h, c_h], axis=1)
                y_h.append(jnp.dot(lhs, rhs, preferred_element_type=F32))
                b_s = (bg_t * dtw_t[h:h + 1, :]).astype(BF16)
                s_h.append(jnp.dot(b_s, x_pair, preferred_element_type=F32))
                e_h.append(e_last[:, h:h + 1])
            ys.append(jnp.where(first_half, y_h[0], y_h[1]))
            state_ref[pair] = (st * jnp.where(first_half, e_h[0], e_h[1])
                               + jnp.where(first_half, s_h[0], s_h[1]))
    y = jnp.concatenate(ys, axis=1)
    y = y + dskip_ref[...] * xs
    y = y * _silu(z_ref[0].astype(F32))
    gw = D_INNER // SSD_GROUPS
    outs = []
    for g in range(SSD_GROUPS):
        yg = y[:, g * gw:(g + 1) * gw]
        ms = jnp.mean(yg * yg, axis=-1, keepdims=True)
        outs.append(yg * lax.rsqrt(ms + LN_EPS) * normw_ref[:, g * gw:(g + 1) * gw])
    y_ref[0] = jnp.concatenate(outs, axis=1).astype(y_ref.dtype)


def _ssd(proj, dt_raw, xbc_blk, z_blk, conv_w, conv_b, dt_bias, a_log, d_skip, norm_w, L):
    b, s, _ = proj.shape
    const = lambda shape: pl.BlockSpec(shape, lambda bi, c: (0,) * len(shape))
    r = jnp.arange((CONV_W - 1) * L, dtype=I32)[:, None]
    shift = (jnp.arange(L + CONV_HALO, dtype=I32)[None, :] == CONV_HALO + r % L - (r // L + 1)).astype(BF16)
    return pl.pallas_call(
        functools.partial(_ssd_kernel, L=L),
        grid=(b, s // L),
        in_specs=[pl.BlockSpec((1, L, CONV_DIM), lambda bi, c: (bi, c, xbc_blk)),
                  pl.BlockSpec((1, L, LANES), lambda bi, c: (bi, c, 0)),
                  pl.BlockSpec((1, L, D_INNER), lambda bi, c: (bi, c, z_blk)),
                  const(((CONV_W - 1) * L, L + CONV_HALO)), const((CONV_W, CONV_DIM)), const((1, CONV_DIM)), const((1, LANES)), const((1, LANES)),
                  const((1, D_INNER)), const((1, D_INNER))],
        out_specs=pl.BlockSpec((1, L, D_INNER), lambda bi, c: (bi, c, 0)),
        out_shape=jax.ShapeDtypeStruct((b, s, D_INNER), BF16),
        scratch_shapes=[pltpu.VMEM((L + CONV_HALO, CONV_DIM), BF16),
                        pltpu.VMEM((SSD_HEADS // 2, SSD_STATE, LANES), F32)],
        compiler_params=pltpu.CompilerParams(dimension_semantics=("parallel", "arbitrary"),
                                             vmem_limit_bytes=VMEM_LIMIT),
        name="ssd",
    )(proj, dt_raw, proj, shift, conv_w, conv_b, dt_bias, a_log, d_skip, norm_w)


def _merge_kernel(attn_ref, y_ref, ga_ref, gs_ref, x_ref, wa_ref, ws_ref, wo_ref, g_ref, b_ref, o_ref, op_ref,
                  *, tm):
    a = jnp.dot(attn_ref[...], wa_ref[...], preferred_element_type=F32)
    s = jnp.dot(y_ref[...], ws_ref[...], preferred_element_type=F32)
    merged = _sigmoid(ga_ref[...].astype(F32)) * a + _sigmoid(gs_ref[...].astype(F32)) * s
    out = jnp.dot(merged.astype(BF16), wo_ref[...], preferred_element_type=F32)
    h = _layer_norm(DN_ALPHA * x_ref[...] + out, g_ref[...], b_ref[...])
    _store_token_tiles(o_ref, h, tm)
    _store_packed_tokens(op_ref, h, tm)


def _merge(attn2d, y2d, proj2d, gate_blk, x2d, wa, ws, wo, g, b, tm):
    t = x2d.shape[0]
    const = lambda shape: pl.BlockSpec(shape, lambda i: (0,) * len(shape))
    return pl.pallas_call(
        functools.partial(_merge_kernel, tm=tm),
        grid=(t // tm,),
        in_specs=[pl.BlockSpec((tm, DA_WIDTH), lambda i: (i, 0)),
                  pl.BlockSpec((tm, D_INNER), lambda i: (i, 0)),
                  pl.BlockSpec((tm, D_MODEL), lambda i: (i, gate_blk)),
                  pl.BlockSpec((tm, D_MODEL), lambda i: (i, gate_blk + 1)),
                  pl.BlockSpec((tm, D_MODEL), lambda i: (i, 0)),
                  const((DA_WIDTH, D_MODEL)), const((D_INNER, D_MODEL)), const((D_MODEL, D_MODEL)),
                  const((1, D_MODEL)), const((1, D_MODEL))],
        out_specs=[pl.BlockSpec((tm * ROW_TILES, LANES), lambda i: (i, 0)),
                   pl.BlockSpec((tm * PACK_ROWS, LANES), lambda i: (i, 0))],
        out_shape=[jax.ShapeDtypeStruct((t * ROW_TILES, LANES), F32),
                   jax.ShapeDtypeStruct((t * PACK_ROWS, LANES), U32)],
        compiler_params=pltpu.CompilerParams(dimension_semantics=("parallel",), vmem_limit_bytes=VMEM_LIMIT),
        name="merge_ln1",
    )(attn2d, y2d, proj2d, proj2d, x2d, wa, ws, wo, g, b)


def _router_kernel(h_ref, w_ref, bias_ref, idx_ref, wgt_ref, rank_ref, cnt_ref, carry_ref, *, tt):
    i = pl.program_id(0)

    @pl.when(i == 0)
    def _():
        carry_ref[...] = jnp.zeros(carry_ref.shape, F32)

    h = _load_token_tiles(h_ref, tt)
    h_hi = h.astype(BF16)
    h_lo = (h - h_hi.astype(F32)).astype(BF16)
    w = w_ref[...]
    w_hi = w.astype(BF16)
    w_lo = (w - w_hi.astype(F32)).astype(BF16)
    logits = (lax.dot_general(w_hi, h_hi, NT_DIMS, preferred_element_type=F32)
              + lax.dot_general(w_hi, h_lo, NT_DIMS, preferred_element_type=F32)
              + lax.dot_general(w_lo, h_hi, NT_DIMS, preferred_element_type=F32))
    scores = _sigmoid(logits)
    choice = scores + bias_ref[...]

    gsz = N_EXPERTS // N_EXPERT_GROUPS
    iota_g = lax.broadcasted_iota(I32, (gsz, tt), 0)
    rows = []
    for g in range(N_EXPERT_GROUPS):
        cg = choice[g * gsz:(g + 1) * gsz, :]
        m1 = jnp.max(cg, axis=0, keepdims=True)
        i1 = jnp.min(jnp.where(cg == m1, iota_g, gsz), axis=0, keepdims=True)
        m2 = jnp.max(jnp.where(iota_g == i1, NEG, cg), axis=0, keepdims=True)
        rows.append(m1 + m2)
    gscore = jnp.concatenate(rows, axis=0)

    iota_grp = lax.broadcasted_iota(I32, (N_EXPERT_GROUPS, tt), 0)
    sel = jnp.zeros((N_EXPERT_GROUPS, tt), F32)
    cur = gscore
    for _ in range(TOPK_GROUPS):
        m = jnp.max(cur, axis=0, keepdims=True)
        ig = jnp.min(jnp.where(cur == m, iota_grp, N_EXPERT_GROUPS), axis=0, keepdims=True)
        hit = iota_grp == ig
        sel = jnp.where(hit, 1.0, sel)
        cur = jnp.where(hit, NEG, cur)
    sel_e = jnp.concatenate([jnp.broadcast_to(sel[g:g + 1, :], (gsz, tt)) for g in range(N_EXPERT_GROUPS)], axis=0)
    cur = jnp.where(sel_e > 0.5, choice, NEG)

    iota_e = lax.broadcasted_iota(I32, (N_EXPERTS, tt), 0)
    idx_rows, w_rows, hits = [], [], []
    for _ in range(TOP_K):
        m = jnp.max(cur, axis=0, keepdims=True)
        ik = jnp.min(jnp.where(cur == m, iota_e, N_EXPERTS), axis=0, keepdims=True)
        hit = iota_e == ik
        w_rows.append(jnp.sum(jnp.where(hit, scores, 0.0), axis=0, keepdims=True))
        cur = jnp.where(hit, NEG, cur)
        idx_rows.append(ik)
        hits.append(hit)
    wv = jnp.concatenate(w_rows, axis=0)
    wv = wv / jnp.sum(wv, axis=0, keepdims=True) * ROUTED_SCALE
    idx_ref[...] = jnp.concatenate(idx_rows, axis=0)
    wgt_ref[...] = wv

    onehot = jnp.zeros((N_EXPERTS, tt), F32)
    for hit in hits:
        onehot = jnp.where(hit, 1.0, onehot)
    onehot_b = onehot.astype(BF16)
    ti = lax.broadcasted_iota(I32, (tt, tt), 0)
    tj = lax.broadcasted_iota(I32, (tt, tt), 1)
    before = jnp.where(ti < tj, 1.0, 0.0).astype(BF16)
    prefix = jnp.dot(onehot_b, before, preferred_element_type=F32) + carry_ref[...]
    rank_rows = [jnp.sum(jnp.where(hit, prefix, 0.0), axis=0, keepdims=True) for hit in hits]
    rank_ref[...] = jnp.concatenate(rank_rows, axis=0).astype(I32)
    total = jnp.dot(onehot_b, jnp.ones((tt, tt), BF16), preferred_element_type=F32)
    new_carry = carry_ref[...] + total
    carry_ref[...] = new_carry
    cnt_ref[...] = new_carry[:, 0:LANES]


def _router(h_tiles, w_router_t, bias_col, tt):
    t = h_tiles.shape[0] // ROW_TILES
    kt = pl.BlockSpec((TOP_K, tt), lambda i: (0, i))
    return pl.pallas_call(
        functools.partial(_router_kernel, tt=tt),
        grid=(t // tt,),
        in_specs=[pl.BlockSpec((tt * ROW_TILES, LANES), lambda i: (i, 0)),
                  pl.BlockSpec((N_EXPERTS, D_MODEL), lambda i: (0, 0)),
                  pl.BlockSpec((N_EXPERTS, 1), lambda i: (0, 0))],
        out_specs=[kt, kt, kt, pl.BlockSpec((N_EXPERTS, LANES), lambda i: (0, 0))],
        out_shape=[jax.ShapeDtypeStruct((TOP_K, t), I32), jax.ShapeDtypeStruct((TOP_K, t), F32),
                   jax.ShapeDtypeStruct((TOP_K, t), I32), jax.ShapeDtypeStruct((N_EXPERTS, LANES), F32)],
        scratch_shapes=[pltpu.VMEM((N_EXPERTS, tt), F32)],
        compiler_params=pltpu.CompilerParams(dimension_semantics=("arbitrary",), vmem_limit_bytes=VMEM_LIMIT),
        name="moe_router",
    )(h_tiles, w_router_t, bias_col)


def _pos_kernel(idx_ref, rank_ref, base_ref, pos_ref, *, tt):
    iota_e = lax.broadcasted_iota(I32, (N_EXPERTS, tt), 0)
    base = base_ref[...]
    rows = []
    for k in range(TOP_K):
        hit = iota_e == idx_ref[k:k + 1, :]
        rows.append(jnp.sum(jnp.where(hit, base, 0.0), axis=0, keepdims=True))
    pos_ref[...] = jnp.concatenate(rows, axis=0).astype(I32) + rank_ref[...]


def _positions(idx_t, rank_t, base_col, tt):
    t = idx_t.shape[1]
    kt = pl.BlockSpec((TOP_K, tt), lambda i: (0, i))
    return pl.pallas_call(
        functools.partial(_pos_kernel, tt=tt),
        grid=(t // tt,),
        in_specs=[kt, kt, pl.BlockSpec((N_EXPERTS, 1), lambda i: (0, 0))],
        out_specs=kt,
        out_shape=jax.ShapeDtypeStruct((TOP_K, t), I32),
        compiler_params=pltpu.CompilerParams(dimension_semantics=("parallel",)),
        name="moe_positions",
    )(idx_t, rank_t, base_col)


def _dispatch_kernel(pos_ref, h_ref, xs_ref, sem, *, tt):
    def issue(t, carry):
        for k in range(TOP_K):
            pltpu.make_async_copy(h_ref.at[_packed_rows(t * PACK_ROWS), :],
                                  xs_ref.at[_packed_rows(pos_ref[k, t] * PACK_ROWS), :], sem).start(priority=k % 2)
        return carry

    lax.fori_loop(0, tt, issue, 0)
    done = xs_ref.at[pl.ds(0, tt * TOP_K * PACK_ROWS), :]
    pltpu.make_async_copy(done, done, sem).wait()


def _dispatch(pos_t, h_packed, n_rows, tt):
    t = pos_t.shape[1]
    return pl.pallas_call(
        functools.partial(_dispatch_kernel, tt=tt),
        grid=(t // tt,),
        in_specs=[pl.BlockSpec((TOP_K, tt), lambda i: (0, i), memory_space=pltpu.SMEM),
                  pl.BlockSpec((tt * PACK_ROWS, LANES), lambda i: (i, 0))],
        out_specs=pl.BlockSpec(memory_space=pl.ANY),
        out_shape=jax.ShapeDtypeStruct((n_rows * PACK_ROWS, LANES), U32),
        scratch_shapes=[pltpu.SemaphoreType.DMA(())],
        compiler_params=pltpu.CompilerParams(dimension_semantics=("arbitrary",), has_side_effects=True),
        name="moe_dispatch",
    )(pos_t, h_packed)


def _expert_kernel(be_ref, nv_ref, nu_ref, xs_ref, wg_ref, wu_ref, wd_ref, ys_ref, wgb_ref, wub_ref, wdb_ref,
                   *, blk):
    j = pl.program_id(0)

    @pl.when(j < nu_ref[0])
    def _():
        @pl.when((j == 0) | (be_ref[j] != be_ref[jnp.maximum(j - 1, 0)]))
        def _():
            wgb_ref[...] = wg_ref[0].astype(BF16)
            wub_ref[...] = wu_ref[0].astype(BF16)
            wdb_ref[...] = wd_ref[0].astype(BF16)

        x = _load_packed_tokens(xs_ref, blk).astype(BF16)
        g = jnp.dot(x, wgb_ref[...], preferred_element_type=F32)
        u = jnp.dot(x, wub_ref[...], preferred_element_type=F32)
        valid = lax.broadcasted_iota(I32, (blk, 1), 0) < nv_ref[j]
        hmid = jnp.where(valid, _silu(g) * u, 0.0).astype(BF16)
        _store_packed_tokens(ys_ref, jnp.dot(hmid, wdb_ref[...], preferred_element_type=F32), blk)


def _experts(blk_expert, blk_valid, n_used, xs, wg, wu, wd, blk):
    n_blocks = blk_expert.shape[0]
    rows = lambda j, be, nv, nu: (jnp.minimum(j, nu[0] - 1), 0)
    wsel = lambda j, be, nv, nu: (be[j], 0, 0)
    return pl.pallas_call(
        functools.partial(_expert_kernel, blk=blk),
        grid_spec=pltpu.PrefetchScalarGridSpec(
            num_scalar_prefetch=3,
            grid=(n_blocks,),
            in_specs=[pl.BlockSpec((blk * PACK_ROWS, LANES), rows),
                      pl.BlockSpec((1, D_MODEL, D_EXPERT), wsel),
                      pl.BlockSpec((1, D_MODEL, D_EXPERT), wsel),
                      pl.BlockSpec((1, D_EXPERT, D_MODEL), wsel)],
            out_specs=pl.BlockSpec((blk * PACK_ROWS, LANES), rows),
            scratch_shapes=[pltpu.VMEM((D_MODEL, D_EXPERT), BF16), pltpu.VMEM((D_MODEL, D_EXPERT), BF16),
                            pltpu.VMEM((D_EXPERT, D_MODEL), BF16)]),
        out_shape=jax.ShapeDtypeStruct(xs.shape, U32),
        compiler_params=pltpu.CompilerParams(dimension_semantics=("arbitrary",), vmem_limit_bytes=VMEM_LIMIT),
        name="moe_experts",
    )(blk_expert, blk_valid, n_used, xs, wg, wu, wd)


COMBINE_CHUNK = TOP_K * PACK_ROWS


def _combine_kernel(pos_ref, posn_ref, wgt_ref, h_ref, wsg_ref, wsu_ref, wsd_ref, g_ref, b_ref, ys_ref, o_ref,
                    buf_ref, base_ref, sem, *, tc):
    i = pl.program_id(0)
    last = pl.num_programs(0) - 1
    ch = COMBINE_CHUNK

    def gather(p_ref, t, dst):
        for k in range(TOP_K):
            pltpu.make_async_copy(ys_ref.at[_packed_rows(p_ref[k, t] * PACK_ROWS), :],
                                  buf_ref.at[dst, _packed_rows((k * tc + t) * PACK_ROWS), :],
                                  sem.at[dst]).start(priority=k % 2)

    @pl.when(i == 0)
    def _():
        def first(t, carry):
            gather(pos_ref, t, 0)
            return carry

        lax.fori_loop(0, tc, first, 0)

    h = _load_token_tiles(h_ref, tc)
    hb = h.astype(BF16)
    g = jnp.dot(hb, wsg_ref[...], preferred_element_type=F32)
    u = jnp.dot(hb, wsu_ref[...], preferred_element_type=F32)
    shared = jnp.dot((_silu(g) * u).astype(BF16), wsd_ref[...], preferred_element_type=F32)
    base_ref[...] = DN_ALPHA * h + shared

    def reduce_tile(cur, nxt):
        done = buf_ref.at[cur]
        pltpu.make_async_copy(done, done, sem.at[cur]).wait()

        def chunk(c, carry):
            t0 = pl.multiple_of(c * ch, ch)
            wgt = wgt_ref[pl.ds(t0, ch), :]
            acc = [None] * ROW_TILES
            for k in range(TOP_K):
                wk = wgt[:, k:k + 1]
                for q in range(PACK_ROWS):
                    w = buf_ref[cur, pl.ds((k * tc + t0) * PACK_ROWS + q, ch, stride=PACK_ROWS), :]
                    lo = wk * pltpu.bitcast(w << 16, F32)
                    hi = wk * pltpu.bitcast(w & jnp.uint32(0xFFFF0000), F32)
                    acc[2 * q] = lo if acc[2 * q] is None else acc[2 * q] + lo
                    acc[2 * q + 1] = hi if acc[2 * q + 1] is None else acc[2 * q + 1] + hi
                    gather(posn_ref, t0 + k * PACK_ROWS + q, nxt)
            routed = jnp.concatenate(acc, axis=-1)
            o_ref[pl.ds(t0, ch), :] = _layer_norm(base_ref[pl.ds(t0, ch), :] + routed, g_ref[...], b_ref[...])
            return carry

        lax.fori_loop(0, tc // ch, chunk, 0)

        @pl.when(i == last)
        def _():
            oth = buf_ref.at[nxt]
            pltpu.make_async_copy(oth, oth, sem.at[nxt]).wait()

    @pl.when(i % 2 == 0)
    def _():
        reduce_tile(0, 1)

    @pl.when(i % 2 == 1)
    def _():
        reduce_tile(1, 0)


def _combine(pos_t, wgt, h_tiles, wsg, wsu, wsd, g, b, ys, tc):
    t = pos_t.shape[1]
    n_tiles = t // tc
    const = lambda shape: pl.BlockSpec(shape, lambda i: (0,) * len(shape))
    return pl.pallas_call(
        functools.partial(_combine_kernel, tc=tc),
        grid=(n_tiles,),
        in_specs=[pl.BlockSpec((TOP_K, tc), lambda i: (0, i), memory_space=pltpu.SMEM),
                  pl.BlockSpec((TOP_K, tc), lambda i: (0, jnp.minimum(i + 1, n_tiles - 1)), memory_space=pltpu.SMEM),
                  pl.BlockSpec((tc, TOP_K), lambda i: (i, 0)),
                  pl.BlockSpec((tc * ROW_TILES, LANES), lambda i: (i, 0)),
                  const((D_MODEL, D_EXPERT)), const((D_MODEL, D_EXPERT)), const((D_EXPERT, D_MODEL)),
                  const((1, D_MODEL)), const((1, D_MODEL)),
                  pl.BlockSpec(memory_space=pl.ANY)],
        out_specs=pl.BlockSpec((tc, D_MODEL), lambda i: (i, 0)),
        out_shape=jax.ShapeDtypeStruct((t, D_MODEL), F32),
        scratch_shapes=[pltpu.VMEM((2, TOP_K * tc * PACK_ROWS, LANES), U32), pltpu.VMEM((tc, D_MODEL), F32),
                        pltpu.SemaphoreType.DMA((2,))],
        compiler_params=pltpu.CompilerParams(dimension_semantics=("arbitrary",), vmem_limit_bytes=VMEM_LIMIT),
        name="moe_combine_ln2",
    )(pos_t, pos_t, wgt, h_tiles, wsg, wsu, wsd, g, b, ys)


def _pick(n, pref):
    t = min(n, pref)
    assert n % t == 0, (n, t)
    return t


def _layer(x, w_in, lq1, lk1, lq2, lk2, subw, conv_w, conv_b, dt_bias, a_log, d_skip, ssd_norm_w,
           w_br_attn, w_br_ssd, w_out, ln1_g, ln1_b, w_router, router_bias, w_eg, w_eu, w_ed,
           w_sg, w_su, w_sd, ln2_g, ln2_b, lam_init):
    b, s, d = x.shape
    t = b * s
    assert d == D_MODEL and s % SSD_CHUNK == 0
    x2d = x.reshape(t, d)

    o_q, o_z, o_xbc = 0, 3 * DA_WIDTH, 3 * DA_WIDTH + D_INNER
    o_dt = o_xbc + CONV_DIM
    o_g = o_dt + SSD_HEADS
    w_all = jnp.concatenate([w_in[:, o_xbc:o_dt], w_in[:, o_q:o_z], w_in[:, o_z:o_xbc],
                             w_in[:, o_g:o_g + 2 * D_MODEL]], axis=1).astype(BF16)
    w_dt = jnp.pad(w_in[:, o_dt:o_g], ((0, 0), (0, LANES - SSD_HEADS))).astype(BF16)
    xbc_blk, q_col0 = 0, CONV_DIM // LANES
    z_blk = (CONV_DIM + 3 * DA_WIDTH) // D_INNER
    gate_blk = (CONV_DIM + 3 * DA_WIDTH + D_INNER) // D_MODEL
    n_proj = w_all.shape[1]
    proj, dt_raw = _proj(x2d, w_all, w_dt, _pick(t, 2048), 2048)
    proj3 = proj.reshape(b, s, n_proj)

    attn = _attention(proj3, q_col0, lq1[None], lk1[None], lq2[None], lk2[None], subw[None],
                      lam_init, _pick(s, 512))

    pad = LANES - SSD_HEADS
    y = _ssd(proj3, dt_raw.reshape(b, s, LANES), xbc_blk, z_blk, conv_w, conv_b[None],
             jnp.pad(dt_bias, (0, pad))[None], jnp.pad(a_log, (0, pad))[None],
             jnp.repeat(d_skip, SSD_HEAD_DIM)[None], ssd_norm_w[None], SSD_CHUNK)

    h1, h1p = _merge(attn.reshape(t, DA_WIDTH), y.reshape(t, D_INNER), proj, gate_blk, x2d, w_br_attn.astype(BF16),
                w_br_ssd.astype(BF16), w_out.astype(BF16), ln1_g[None], ln1_b[None], _pick(t, 512))

    tt = _pick(t, 512)
    idx_t, wgt_t, rank_t, cnt = _router(h1, w_router.T, router_bias[:, None], tt)
    blk = MOE_ROW_BLOCK
    n_blocks = (t * TOP_K) // blk + N_EXPERTS
    counts = cnt[:, 0].astype(I32)
    blocks_per_e = (counts + blk - 1) // blk
    blk_end = jnp.cumsum(blocks_per_e)
    blk_start = blk_end - blocks_per_e
    n_used = blk_end[-1:]
    jblk = jnp.minimum(jnp.arange(n_blocks, dtype=I32), n_used[0] - 1)
    blk_expert = jnp.minimum(jnp.sum((blk_end[None, :] <= jblk[:, None]).astype(I32), axis=1), N_EXPERTS - 1)
    blk_valid = jnp.clip(counts[blk_expert] - (jblk - blk_start[blk_expert]) * blk, 0, blk).astype(I32)
    base_col = (blk_start * blk).astype(F32)[:, None]
    pos_t = _positions(idx_t, rank_t, base_col, tt)

    xs = _dispatch(pos_t, h1p, n_blocks * blk, tt)
    ys = _experts(blk_expert, blk_valid, n_used.astype(I32), xs, w_eg, w_eu, w_ed, blk)
    out = _combine(pos_t, wgt_t.T, h1, w_sg.astype(BF16), w_su.astype(BF16), w_sd.astype(BF16),
                   ln2_g[None], ln2_b[None], ys, _pick(t, 512))
    return out.reshape(b, s, d)


def kernel(x, w_in, lambda_q1, lambda_k1, lambda_q2, lambda_k2, attn_subln_w, conv_w, conv_b, dt_bias, a_log,
           d_skip, ssd_norm_w, w_br_attn, w_br_ssd, w_out, ln1_g, ln1_b, w_router, router_bias, w_exp_gate,
           w_exp_up, w_exp_down, w_sh_gate, w_sh_up, w_sh_down, ln2_g, ln2_b):
    h = x
    for layer in range(DEPTH):
        lam_init = 0.8 - 0.6 * math.exp(-0.3 * layer)
        h = _layer(h, w_in[layer], lambda_q1[layer], lambda_k1[layer], lambda_q2[layer], lambda_k2[layer],
                   attn_subln_w[layer], conv_w[layer], conv_b[layer], dt_bias[layer], a_log[layer],
                   d_skip[layer], ssd_norm_w[layer], w_br_attn[layer], w_br_ssd[layer], w_out[layer],
                   ln1_g[layer], ln1_b[layer], w_router[layer], router_bias[layer], w_exp_gate[layer],
                   w_exp_up[layer], w_exp_down[layer], w_sh_gate[layer], w_sh_up[layer], w_sh_down[layer],
                   ln2_g[layer], ln2_b[layer], lam_init)
    return h
```

```python
import functools
import math

import jax
import jax.numpy as jnp
from jax import lax
from jax.experimental import pallas as pl
from jax.experimental.pallas import tpu as pltpu

F32 = jnp.float32
BF16 = jnp.bfloat16
I32 = jnp.int32

D_MODEL = 1024
DEPTH = 1
DA_HEADS = 8
DA_HEAD_DIM = 64
DA_WIDTH = DA_HEADS * 2 * DA_HEAD_DIM
D_INNER = 2048
SSD_HEAD_DIM = 64
SSD_HEADS = D_INNER // SSD_HEAD_DIM
SSD_GROUPS = 4
SSD_STATE = 128
CONV_W = 4
CONV_DIM = D_INNER + 2 * SSD_GROUPS * SSD_STATE
SSD_CHUNK = 128
N_EXPERTS = 256
TOP_K = 8
N_EXPERT_GROUPS = 8
TOPK_GROUPS = 4
D_EXPERT = 256
ROUTED_SCALE = 2.5
DN_ALPHA = (2.0 * DEPTH) ** 0.25
LN_EPS = 1e-5

LANES = 128
SUBLANES = 8
ROW_TILES = D_MODEL // LANES
NEG = -1e30
MOE_ROW_BLOCK = 1024
VMEM_LIMIT = 56 * 1024 * 1024

NT_DIMS = (((1,), (1,)), ((), ()))


def _sigmoid(v):
    return 0.5 * jnp.tanh(0.5 * v) + 0.5


def _silu(v):
    hv = 0.5 * v
    return hv * jnp.tanh(hv) + hv


def _layer_norm(v, g, b):
    mu = jnp.mean(v, axis=-1, keepdims=True)
    d = v - mu
    var = jnp.mean(d * d, axis=-1, keepdims=True)
    return d * lax.rsqrt(var + LN_EPS) * g + b


def _load_token_tiles(ref, n_tok, row0=0):
    return jnp.concatenate(
        [ref[pl.ds(row0 + s, n_tok, stride=ROW_TILES), :] for s in range(ROW_TILES)], axis=-1)


def _store_token_tiles(ref, val, n_tok):
    for s in range(ROW_TILES):
        ref[pl.ds(s, n_tok, stride=ROW_TILES), :] = val[:, s * LANES:(s + 1) * LANES]


PACK_ROWS = ROW_TILES // 2
U32 = jnp.uint32


def _packed_rows(row0):
    return pl.ds(pl.multiple_of(row0, PACK_ROWS), PACK_ROWS)


def _store_packed_tokens(ref, val, n_tok):
    for q in range(PACK_ROWS):
        lo = pltpu.bitcast(val[:, (2 * q) * LANES:(2 * q + 1) * LANES].astype(BF16).astype(F32), U32)
        hi = pltpu.bitcast(val[:, (2 * q + 1) * LANES:(2 * q + 2) * LANES].astype(BF16).astype(F32), U32)
        ref[pl.ds(q, n_tok, stride=PACK_ROWS), :] = (lo >> 16) | hi


def _load_packed_tokens(ref, n_tok, tok0=0):
    chunks = []
    for q in range(PACK_ROWS):
        w = ref[pl.ds(tok0 * PACK_ROWS + q, n_tok, stride=PACK_ROWS), :]
        chunks.append(pltpu.bitcast(w << 16, F32))
        chunks.append(pltpu.bitcast(w & jnp.uint32(0xFFFF0000), F32))
    return jnp.concatenate(chunks, axis=-1)


def _split3(v):
    hi = v.astype(BF16)
    r1 = v - hi.astype(F32)
    mid = r1.astype(BF16)
    lo = (r1 - mid.astype(F32)).astype(BF16)
    return hi, mid, lo


def _proj_kernel(x_ref, w_ref, wdt_ref, o_ref, dt_ref, xb_ref):
    @pl.when(pl.program_id(1) == 0)
    def _():
        xb = x_ref[...].astype(BF16)
        xb_ref[...] = xb
        dt_ref[...] = jnp.dot(xb, wdt_ref[...], preferred_element_type=F32)

    o_ref[...] = jnp.dot(xb_ref[...], w_ref[...], preferred_element_type=F32).astype(o_ref.dtype)


def _proj(x2d, w, w_dt, tm, tn):
    t, k = x2d.shape
    n = w.shape[1]
    return pl.pallas_call(
        _proj_kernel,
        grid=(t // tm, n // tn),
        in_specs=[pl.BlockSpec((tm, k), lambda i, j: (i, 0)),
                  pl.BlockSpec((k, tn), lambda i, j: (0, j)),
                  pl.BlockSpec((k, LANES), lambda i, j: (0, 0))],
        out_specs=[pl.BlockSpec((tm, tn), lambda i, j: (i, j)),
                   pl.BlockSpec((tm, LANES), lambda i, j: (i, 0))],
        out_shape=[jax.ShapeDtypeStruct((t, n), BF16), jax.ShapeDtypeStruct((t, LANES), F32)],
        scratch_shapes=[pltpu.VMEM((tm, k), BF16)],
        compiler_params=pltpu.CompilerParams(dimension_semantics=("parallel", "arbitrary"),
                                             vmem_limit_bytes=VMEM_LIMIT),
        name="proj",
    )(x2d, w, w_dt)


ATTN_ROW_CHUNK = 64
ATTN_Q_TILE = 2048


def _attn_kernel(lq1_ref, lk1_ref, lq2_ref, lk2_ref, subw_ref, q_ref, k_ref, v_ref, o_ref,
                 qs_ref, m_ref, l_ref, acc_ref, *, tq, tk, lam_init):
    i = pl.program_id(2)
    rows = 2 * tq
    ch = ATTN_ROW_CHUNK
    q = q_ref[0].astype(F32) * (DA_HEAD_DIM ** -0.5 * math.log2(math.e))
    lane = lax.broadcasted_iota(I32, (tq, LANES), 1)
    qs_ref[0:tq, :] = jnp.where(lane < DA_HEAD_DIM, q, 0.0).astype(BF16)
    qs_ref[tq:rows, :] = jnp.where(lane >= DA_HEAD_DIM, q, 0.0).astype(BF16)
    m_ref[...] = jnp.full((rows, LANES), NEG, F32)
    l_ref[...] = jnp.zeros((rows, LANES), F32)
    acc_ref[...] = jnp.zeros((rows, LANES), F32)
    qs = qs_ref[...]

    def step(j, q0, diagonal):
        start = pl.multiple_of(j * tk, tk)
        kb = k_ref[0, pl.ds(start, tk), :]
        vb = v_ref[0, pl.ds(start, tk), :]
        nq = tq - q0
        q_sub = qs if q0 == 0 else jnp.concatenate([qs[q0:tq, :], qs[tq + q0:rows, :]], axis=0)
        s = lax.dot_general(q_sub, kb, NT_DIMS, preferred_element_type=F32)
        ps, alphas = [], []
        for c in range(2 * nq // ch):
            ql = q0 + (c * ch) % nq
            g0 = (c * ch) // nq * tq + ql
            sc = s[c * ch:(c + 1) * ch, :]
            if diagonal and ql < q0 + tk:
                row = (ql - q0) + lax.broadcasted_iota(I32, (ch, tk), 0)
                col = lax.broadcasted_iota(I32, (ch, tk), 1)
                sc = jnp.where(col <= row, sc, NEG)
            m_prev = m_ref[g0:g0 + ch, :]
            m_new = jnp.maximum(m_prev, jnp.max(sc, axis=-1, keepdims=True))
            alpha = jnp.exp2(m_prev - m_new)
            p = jnp.exp2(sc - jnp.concatenate([m_new] * (tk // LANES), axis=1))
            psum = p[:, 0:LANES]
            for t in range(1, tk // LANES):
                psum = psum + p[:, t * LANES:(t + 1) * LANES]
            l_ref[g0:g0 + ch, :] = alpha * l_ref[g0:g0 + ch, :] + psum
            m_ref[g0:g0 + ch, :] = m_new
            ps.append(p.astype(BF16))
            alphas.append(alpha)
        pv = jnp.dot(jnp.concatenate(ps, axis=0), vb, preferred_element_type=F32)
        alpha_all = jnp.concatenate(alphas, axis=0)
        for mp in range(2):
            dst = slice(mp * tq + q0, (mp + 1) * tq)
            src = slice(mp * nq, (mp + 1) * nq)
            acc_ref[dst, :] = alpha_all[src, :] * acc_ref[dst, :] + pv[src, :]

    per_tile = tq // tk

    def full_block_pair(jj, carry):
        step(2 * jj, 0, False)
        step(2 * jj + 1, 0, False)
        return carry

    lax.fori_loop(0, i * (per_tile // 2), full_block_pair, 0)
    for d in range(per_tile):
        step(i * per_tile + d, d * tk, True)
    o = acc_ref[...] / jnp.sum(l_ref[...], axis=-1, keepdims=True)
    lam = (jnp.exp(jnp.sum(lq1_ref[...] * lk1_ref[...], axis=-1, keepdims=True))
           - jnp.exp(jnp.sum(lq2_ref[...] * lk2_ref[...], axis=-1, keepdims=True)) + lam_init)
    d = o[0:tq, :] - lam * o[tq:rows, :]
    ms = jnp.mean(d * d, axis=-1, keepdims=True)
    o_ref[0] = (d * lax.rsqrt(ms + LN_EPS) * subw_ref[...] * (1.0 - lam_init)).astype(o_ref.dtype)


def _attention(proj, col0, lq1, lk1, lq2, lk2, subw, lam_init, tq, tk):
    b, s, _ = proj.shape
    vec = pl.BlockSpec((1, DA_HEAD_DIM), lambda bi, h, i: (0, 0))
    return pl.pallas_call(
        functools.partial(_attn_kernel, tq=tq, tk=tk, lam_init=lam_init),
        grid=(b, DA_HEADS, s // tq),
        in_specs=[vec, vec, vec, vec,
                  pl.BlockSpec((1, LANES), lambda bi, h, i: (0, 0)),
                  pl.BlockSpec((1, tq, LANES), lambda bi, h, i: (bi, i, col0 + h)),
                  pl.BlockSpec((1, s, LANES), lambda bi, h, i: (bi, 0, col0 + DA_HEADS + h)),
                  pl.BlockSpec((1, s, LANES), lambda bi, h, i: (bi, 0, col0 + 2 * DA_HEADS + h))],
        out_specs=pl.BlockSpec((1, tq, LANES), lambda bi, h, i: (bi, i, h)),
        out_shape=jax.ShapeDtypeStruct((b, s, DA_WIDTH), BF16),
        scratch_shapes=[pltpu.VMEM((2 * tq, LANES), BF16), pltpu.VMEM((2 * tq, LANES), F32),
                        pltpu.VMEM((2 * tq, LANES), F32), pltpu.VMEM((2 * tq, LANES), F32)],
        compiler_params=pltpu.CompilerParams(dimension_semantics=("parallel", "parallel", "arbitrary"),
                                             vmem_limit_bytes=VMEM_LIMIT),
        name="diff_attn",
    )(lq1, lk1, lq2, lk2, subw, proj, proj, proj)


CONV_HALO = 16


def _ssd_kernel(xbc_ref, dt_ref, z_ref, shift_ref, convw_ref, convb_ref, dtb_ref, alog_ref, dskip_ref, normw_ref,
                y_ref, ext_ref, state_ref, *, L):
    c = pl.program_id(1)
    halo = CONV_HALO

    @pl.when(c == 0)
    def _():
        ext_ref[0:halo, :] = jnp.zeros((halo, CONV_DIM), BF16)
        state_ref[...] = jnp.zeros(state_ref.shape, F32)

    ext_ref[halo:halo + L, :] = xbc_ref[0]
    delayed = jnp.dot(shift_ref[...], ext_ref[...], preferred_element_type=F32)
    acc = convb_ref[...] + convw_ref[CONV_W - 1:CONV_W, :] * xbc_ref[0].astype(F32)
    for d in range(1, CONV_W):
        acc = acc + convw_ref[CONV_W - 1 - d:CONV_W - d, :] * delayed[(d - 1) * L:d * L, :]
    ext_ref[0:halo, :] = ext_ref[L:L + halo, :]
    xbc = _silu(acc)
    xs = xbc[:, 0:D_INNER]
    bm = xbc[:, D_INNER:D_INNER + SSD_GROUPS * SSD_STATE]
    cm = xbc[:, D_INNER + SSD_GROUPS * SSD_STATE:CONV_DIM]

    dt_in = dt_ref[0] + dtb_ref[...]
    dt = jnp.maximum(dt_in, 0.0) + jnp.log1p(jnp.exp(-jnp.abs(dt_in)))
    a = -jnp.exp(alog_ref[...]) * math.log2(math.e)
    da = dt * a
    row = lax.broadcasted_iota(I32, (L, L), 0)
    col = lax.broadcasted_iota(I32, (L, L), 1)
    tri = row >= col
    tril = jnp.where(tri, 1.0, 0.0).astype(BF16)
    hi, mid, lo = _split3(da)
    a_cum = (jnp.dot(tril, hi, preferred_element_type=F32) + jnp.dot(tril, mid, preferred_element_type=F32)
             + jnp.dot(tril, lo, preferred_element_type=F32))
    a_last = a_cum[L - 1:L, :]
    a_cum_t = a_cum.T
    dt_t = dt.T
    dtw_t = (dt * jnp.exp2(a_last - a_cum)).T
    ea = jnp.exp2(a_cum)
    e_last = jnp.exp2(a_last)
    lane = lax.broadcasted_iota(I32, (1, LANES), 1)
    first_half = lane < SSD_HEAD_DIM

    heads_per_group = SSD_HEADS // SSD_GROUPS
    pairs_per_group = heads_per_group // 2
    ys = []
    for g in range(SSD_GROUPS):
        bg = bm[:, g * SSD_STATE:(g + 1) * SSD_STATE]
        cg = cm[:, g * SSD_STATE:(g + 1) * SSD_STATE]
        cb = lax.dot_general(cg.astype(BF16), bg.astype(BF16), NT_DIMS, preferred_element_type=F32)
        bg_t = bg.T
        for pp in range(pairs_per_group):
            pair = g * pairs_per_group + pp
            x_pair = xs[:, pair * LANES:(pair + 1) * LANES].astype(BF16)
            st = state_ref[pair]
            rhs = jnp.concatenate([x_pair, st.astype(BF16)], axis=0)
            y_h, s_h, e_h = [], [], []
            for hl in range(2):
                h = g * heads_per_group + 2 * pp + hl
                seg = a_cum[:, h:h + 1] - a_cum_t[h:h + 1, :]
                decay = jnp.exp2(jnp.where(tri, seg, NEG))
                m_h = (cb * decay * dt_t[h:h + 1, :]).astype(BF16)
                c_h = (cg * ea[:, h:h + 1]).astype(BF16)
                lhs = jnp.concatenate([m_h, c_h], axis=1)
                y_h.append(jnp.dot(lhs, rhs, preferred_element_type=F32))
                b_s = (bg_t * dtw_t[h:h + 1, :]).astype(BF16)
                s_h.append(jnp.dot(b_s, x_pair, preferred_element_type=F32))
                e_h.append(e_last[:, h:h + 1])
            ys.append(jnp.where(first_half, y_h[0], y_h[1]))
            state_ref[pair] = (st * jnp.where(first_half, e_h[0], e_h[1])
                               + jnp.where(first_half, s_h[0], s_h[1]))
    y = jnp.concatenate(ys, axis=1)
    y = y + dskip_ref[...] * xs
    y = y * _silu(z_ref[0].astype(F32))
    gw = D_INNER // SSD_GROUPS
    outs = []
    for g in range(SSD_GROUPS):
        yg = y[:, g * gw:(g + 1) * gw]
        ms = jnp.mean(yg * yg, axis=-1, keepdims=True)
        outs.append(yg * lax.rsqrt(ms + LN_EPS) * normw_ref[:, g * gw:(g + 1) * gw])
    y_ref[0] = jnp.concatenate(outs, axis=1).astype(y_ref.dtype)


def _ssd(proj, dt_raw, xbc_blk, z_blk, conv_w, conv_b, dt_bias, a_log, d_skip, norm_w, L):
    b, s, _ = proj.shape
    const = lambda shape: pl.BlockSpec(shape, lambda bi, c: (0,) * len(shape))
    r = jnp.arange((CONV_W - 1) * L, dtype=I32)[:, None]
    shift = (jnp.arange(L + CONV_HALO, dtype=I32)[None, :] == CONV_HALO + r % L - (r // L + 1)).astype(BF16)
    return pl.pallas_call(
        functools.partial(_ssd_kernel, L=L),
        grid=(b, s // L),
        in_specs=[pl.BlockSpec((1, L, CONV_DIM), lambda bi, c: (bi, c, xbc_blk)),
                  pl.BlockSpec((1, L, LANES), lambda bi, c: (bi, c, 0)),
                  pl.BlockSpec((1, L, D_INNER), lambda bi, c: (bi, c, z_blk)),
                  const(((CONV_W - 1) * L, L + CONV_HALO)), const((CONV_W, CONV_DIM)), const((1, CONV_DIM)), const((1, LANES)), const((1, LANES)),
                  const((1, D_INNER)), const((1, D_INNER))],
        out_specs=pl.BlockSpec((1, L, D_INNER), lambda bi, c: (bi, c, 0)),
        out_shape=jax.ShapeDtypeStruct((b, s, D_INNER), BF16),
        scratch_shapes=[pltpu.VMEM((L + CONV_HALO, CONV_DIM), BF16),
                        pltpu.VMEM((SSD_HEADS // 2, SSD_STATE, LANES), F32)],
        compiler_params=pltpu.CompilerParams(dimension_semantics=("parallel", "arbitrary"),
                                             vmem_limit_bytes=VMEM_LIMIT),
        name="ssd",
    )(proj, dt_raw, proj, shift, conv_w, conv_b, dt_bias, a_log, d_skip, norm_w)


def _merge_kernel(attn_ref, y_ref, ga_ref, gs_ref, x_ref, wa_ref, ws_ref, wo_ref, g_ref, b_ref, o_ref, op_ref,
                  *, tm):
    a = jnp.dot(attn_ref[...], wa_ref[...], preferred_element_type=F32)
    s = jnp.dot(y_ref[...], ws_ref[...], preferred_element_type=F32)
    merged = _sigmoid(ga_ref[...].astype(F32)) * a + _sigmoid(gs_ref[...].astype(F32)) * s
    out = jnp.dot(merged.astype(BF16), wo_ref[...], preferred_element_type=F32)
    h = _layer_norm(DN_ALPHA * x_ref[...] + out, g_ref[...], b_ref[...])
    _store_token_tiles(o_ref, h, tm)
    _store_packed_tokens(op_ref, h, tm)


def _merge(attn2d, y2d, proj2d, gate_blk, x2d, wa, ws, wo, g, b, tm):
    t = x2d.shape[0]
    const = lambda shape: pl.BlockSpec(shape, lambda i: (0,) * len(shape))
    return pl.pallas_call(
        functools.partial(_merge_kernel, tm=tm),
        grid=(t // tm,),
        in_specs=[pl.BlockSpec((tm, DA_WIDTH), lambda i: (i, 0)),
                  pl.BlockSpec((tm, D_INNER), lambda i: (i, 0)),
                  pl.BlockSpec((tm, D_MODEL), lambda i: (i, gate_blk)),
                  pl.BlockSpec((tm, D_MODEL), lambda i: (i, gate_blk + 1)),
                  pl.BlockSpec((tm, D_MODEL), lambda i: (i, 0)),
                  const((DA_WIDTH, D_MODEL)), const((D_INNER, D_MODEL)), const((D_MODEL, D_MODEL)),
                  const((1, D_MODEL)), const((1, D_MODEL))],
        out_specs=[pl.BlockSpec((tm * ROW_TILES, LANES), lambda i: (i, 0)),
                   pl.BlockSpec((tm * PACK_ROWS, LANES), lambda i: (i, 0))],
        out_shape=[jax.ShapeDtypeStruct((t * ROW_TILES, LANES), F32),
                   jax.ShapeDtypeStruct((t * PACK_ROWS, LANES), U32)],
        compiler_params=pltpu.CompilerParams(dimension_semantics=("parallel",), vmem_limit_bytes=VMEM_LIMIT),
        name="merge_ln1",
    )(attn2d, y2d, proj2d, proj2d, x2d, wa, ws, wo, g, b)


def _router_kernel(h_ref, w_ref, bias_ref, idx_ref, wgt_ref, rank_ref, cnt_ref, carry_ref, *, tt):
    i = pl.program_id(0)

    @pl.when(i == 0)
    def _():
        carry_ref[...] = jnp.zeros(carry_ref.shape, F32)

    h = _load_token_tiles(h_ref, tt)
    h_hi = h.astype(BF16)
    h_lo = (h - h_hi.astype(F32)).astype(BF16)
    w = w_ref[...]
    w_hi = w.astype(BF16)
    w_lo = (w - w_hi.astype(F32)).astype(BF16)
    logits = (lax.dot_general(w_hi, h_hi, NT_DIMS, preferred_element_type=F32)
              + lax.dot_general(w_hi, h_lo, NT_DIMS, preferred_element_type=F32)
              + lax.dot_general(w_lo, h_hi, NT_DIMS, preferred_element_type=F32))
    scores = _sigmoid(logits)
    choice = scores + bias_ref[...]

    gsz = N_EXPERTS // N_EXPERT_GROUPS
    iota_g = lax.broadcasted_iota(I32, (gsz, tt), 0)
    rows = []
    for g in range(N_EXPERT_GROUPS):
        cg = choice[g * gsz:(g + 1) * gsz, :]
        m1 = jnp.max(cg, axis=0, keepdims=True)
        i1 = jnp.min(jnp.where(cg == m1, iota_g, gsz), axis=0, keepdims=True)
        m2 = jnp.max(jnp.where(iota_g == i1, NEG, cg), axis=0, keepdims=True)
        rows.append(m1 + m2)
    gscore = jnp.concatenate(rows, axis=0)

    iota_grp = lax.broadcasted_iota(I32, (N_EXPERT_GROUPS, tt), 0)
    sel = jnp.zeros((N_EXPERT_GROUPS, tt), F32)
    cur = gscore
    for _ in range(TOPK_GROUPS):
        m = jnp.max(cur, axis=0, keepdims=True)
        ig = jnp.min(jnp.where(cur == m, iota_grp, N_EXPERT_GROUPS), axis=0, keepdims=True)
        hit = iota_grp == ig
        sel = jnp.where(hit, 1.0, sel)
        cur = jnp.where(hit, NEG, cur)
    sel_e = jnp.concatenate([jnp.broadcast_to(sel[g:g + 1, :], (gsz, tt)) for g in range(N_EXPERT_GROUPS)], axis=0)
    cur = jnp.where(sel_e > 0.5, choice, NEG)

    iota_e = lax.broadcasted_iota(I32, (N_EXPERTS, tt), 0)
    idx_rows, w_rows, hits = [], [], []
    for _ in range(TOP_K):
        m = jnp.max(cur, axis=0, keepdims=True)
        ik = jnp.min(jnp.where(cur == m, iota_e, N_EXPERTS), axis=0, keepdims=True)
        hit = iota_e == ik
        w_rows.append(jnp.sum(jnp.where(hit, scores, 0.0), axis=0, keepdims=True))
        cur = jnp.where(hit, NEG, cur)
        idx_rows.append(ik)
        hits.append(hit)
    wv = jnp.concatenate(w_rows, axis=0)
    wv = wv / jnp.sum(wv, axis=0, keepdims=True) * ROUTED_SCALE
    idx_ref[...] = jnp.concatenate(idx_rows, axis=0)
    wgt_ref[...] = wv

    onehot = jnp.zeros((N_EXPERTS, tt), F32)
    for hit in hits:
        onehot = jnp.where(hit, 1.0, onehot)
    onehot_b = onehot.astype(BF16)
    ti = lax.broadcasted_iota(I32, (tt, tt), 0)
    tj = lax.broadcasted_iota(I32, (tt, tt), 1)
    before = jnp.where(ti < tj, 1.0, 0.0).astype(BF16)
    prefix = jnp.dot(onehot_b, before, preferred_element_type=F32) + carry_ref[...]
    rank_rows = [jnp.sum(jnp.where(hit, prefix, 0.0), axis=0, keepdims=True) for hit in hits]
    rank_ref[...] = jnp.concatenate(rank_rows, axis=0).astype(I32)
    total = jnp.dot(onehot_b, jnp.ones((tt, tt), BF16), preferred_element_type=F32)
    new_carry = carry_ref[...] + total
    carry_ref[...] = new_carry
    cnt_ref[...] = new_carry[:, 0:LANES]


def _router(h_tiles, w_router_t, bias_col, tt):
    t = h_tiles.shape[0] // ROW_TILES
    kt = pl.BlockSpec((TOP_K, tt), lambda i: (0, i))
    return pl.pallas_call(
        functools.partial(_router_kernel, tt=tt),
        grid=(t // tt,),
        in_specs=[pl.BlockSpec((tt * ROW_TILES, LANES), lambda i: (i, 0)),
                  pl.BlockSpec((N_EXPERTS, D_MODEL), lambda i: (0, 0)),
                  pl.BlockSpec((N_EXPERTS, 1), lambda i: (0, 0))],
        out_specs=[kt, kt, kt, pl.BlockSpec((N_EXPERTS, LANES), lambda i: (0, 0))],
        out_shape=[jax.ShapeDtypeStruct((TOP_K, t), I32), jax.ShapeDtypeStruct((TOP_K, t), F32),
                   jax.ShapeDtypeStruct((TOP_K, t), I32), jax.ShapeDtypeStruct((N_EXPERTS, LANES), F32)],
        scratch_shapes=[pltpu.VMEM((N_EXPERTS, tt), F32)],
        compiler_params=pltpu.CompilerParams(dimension_semantics=("arbitrary",), vmem_limit_bytes=VMEM_LIMIT),
        name="moe_router",
    )(h_tiles, w_router_t, bias_col)


def _pos_kernel(idx_ref, rank_ref, base_ref, pos_ref, *, tt):
    iota_e = lax.broadcasted_iota(I32, (N_EXPERTS, tt), 0)
    base = base_ref[...]
    rows = []
    for k in range(TOP_K):
        hit = iota_e == idx_ref[k:k + 1, :]
        rows.append(jnp.sum(jnp.where(hit, base, 0.0), axis=0, keepdims=True))
    pos_ref[...] = jnp.concatenate(rows, axis=0).astype(I32) + rank_ref[...]


def _positions(idx_t, rank_t, base_col, tt):
    t = idx_t.shape[1]
    kt = pl.BlockSpec((TOP_K, tt), lambda i: (0, i))
    return pl.pallas_call(
        functools.partial(_pos_kernel, tt=tt),
        grid=(t // tt,),
        in_specs=[kt, kt, pl.BlockSpec((N_EXPERTS, 1), lambda i: (0, 0))],
        out_specs=kt,
        out_shape=jax.ShapeDtypeStruct((TOP_K, t), I32),
        compiler_params=pltpu.CompilerParams(dimension_semantics=("parallel",)),
        name="moe_positions",
    )(idx_t, rank_t, base_col)


def _dispatch_kernel(pos_ref, h_ref, xs_ref, sem, *, tt):
    def issue(t, carry):
        for k in range(TOP_K):
            pltpu.make_async_copy(h_ref.at[_packed_rows(t * PACK_ROWS), :],
                                  xs_ref.at[_packed_rows(pos_ref[k, t] * PACK_ROWS), :], sem).start(priority=k % 2)
        return carry

    lax.fori_loop(0, tt, issue, 0)
    done = xs_ref.at[pl.ds(0, tt * TOP_K * PACK_ROWS), :]
    pltpu.make_async_copy(done, done, sem).wait()


def _dispatch(pos_t, h_packed, n_rows, tt):
    t = pos_t.shape[1]
    return pl.pallas_call(
        functools.partial(_dispatch_kernel, tt=tt),
        grid=(t // tt,),
        in_specs=[pl.BlockSpec((TOP_K, tt), lambda i: (0, i), memory_space=pltpu.SMEM),
                  pl.BlockSpec((tt * PACK_ROWS, LANES), lambda i: (i, 0))],
        out_specs=pl.BlockSpec(memory_space=pl.ANY),
        out_shape=jax.ShapeDtypeStruct((n_rows * PACK_ROWS, LANES), U32),
        scratch_shapes=[pltpu.SemaphoreType.DMA(())],
        compiler_params=pltpu.CompilerParams(dimension_semantics=("arbitrary",), has_side_effects=True),
        name="moe_dispatch",
    )(pos_t, h_packed)


def _expert_kernel(be_ref, nv_ref, nu_ref, xs_ref, wg_ref, wu_ref, wd_ref, ys_ref, wgb_ref, wub_ref, wdb_ref,
                   *, blk):
    j = pl.program_id(0)

    @pl.when(j < nu_ref[0])
    def _():
        @pl.when((j == 0) | (be_ref[j] != be_ref[jnp.maximum(j - 1, 0)]))
        def _():
            wgb_ref[...] = wg_ref[0].astype(BF16)
            wub_ref[...] = wu_ref[0].astype(BF16)
            wdb_ref[...] = wd_ref[0].astype(BF16)

        x = _load_packed_tokens(xs_ref, blk).astype(BF16)
        g = jnp.dot(x, wgb_ref[...], preferred_element_type=F32)
        u = jnp.dot(x, wub_ref[...], preferred_element_type=F32)
        valid = lax.broadcasted_iota(I32, (blk, 1), 0) < nv_ref[j]
        hmid = jnp.where(valid, _silu(g) * u, 0.0).astype(BF16)
        _store_packed_tokens(ys_ref, jnp.dot(hmid, wdb_ref[...], preferred_element_type=F32), blk)


def _experts(blk_expert, blk_valid, n_used, xs, wg, wu, wd, blk):
    n_blocks = blk_expert.shape[0]
    rows = lambda j, be, nv, nu: (jnp.minimum(j, nu[0] - 1), 0)
    wsel = lambda j, be, nv, nu: (be[j], 0, 0)
    return pl.pallas_call(
        functools.partial(_expert_kernel, blk=blk),
        grid_spec=pltpu.PrefetchScalarGridSpec(
            num_scalar_prefetch=3,
            grid=(n_blocks,),
            in_specs=[pl.BlockSpec((blk * PACK_ROWS, LANES), rows),
                      pl.BlockSpec((1, D_MODEL, D_EXPERT), wsel),
                      pl.BlockSpec((1, D_MODEL, D_EXPERT), wsel),
                      pl.BlockSpec((1, D_EXPERT, D_MODEL), wsel)],
            out_specs=pl.BlockSpec((blk * PACK_ROWS, LANES), rows),
            scratch_shapes=[pltpu.VMEM((D_MODEL, D_EXPERT), BF16), pltpu.VMEM((D_MODEL, D_EXPERT), BF16),
                            pltpu.VMEM((D_EXPERT, D_MODEL), BF16)]),
        out_shape=jax.ShapeDtypeStruct(xs.shape, U32),
        compiler_params=pltpu.CompilerParams(dimension_semantics=("arbitrary",), vmem_limit_bytes=VMEM_LIMIT),
        name="moe_experts",
    )(blk_expert, blk_valid, n_used, xs, wg, wu, wd)


COMBINE_CHUNK = TOP_K * PACK_ROWS


def _combine_kernel(pos_ref, posn_ref, wgt_ref, h_ref, wsg_ref, wsu_ref, wsd_ref, g_ref, b_ref, ys_ref, o_ref,
                    buf_ref, base_ref, sem, *, tc):
    i = pl.program_id(0)
    last = pl.num_programs(0) - 1
    ch = COMBINE_CHUNK

    def gather(p_ref, t, dst):
        for k in range(TOP_K):
            pltpu.make_async_copy(ys_ref.at[_packed_rows(p_ref[k, t] * PACK_ROWS), :],
                                  buf_ref.at[dst, _packed_rows((k * tc + t) * PACK_ROWS), :],
                                  sem.at[dst]).start(priority=k % 2)

    @pl.when(i == 0)
    def _():
        def first(t, carry):
            gather(pos_ref, t, 0)
            return carry

        lax.fori_loop(0, tc, first, 0)

    h = _load_token_tiles(h_ref, tc)
    hb = h.astype(BF16)
    g = jnp.dot(hb, wsg_ref[...], preferred_element_type=F32)
    u = jnp.dot(hb, wsu_ref[...], preferred_element_type=F32)
    shared = jnp.dot((_silu(g) * u).astype(BF16), wsd_ref[...], preferred_element_type=F32)
    base_ref[...] = DN_ALPHA * h + shared

    def reduce_tile(cur, nxt):
        done = buf_ref.at[cur]
        pltpu.make_async_copy(done, done, sem.at[cur]).wait()

        def chunk(c, carry):
            t0 = pl.multiple_of(c * ch, ch)
            wgt = wgt_ref[pl.ds(t0, ch), :]
            acc = [None] * ROW_TILES
            for k in range(TOP_K):
                wk = wgt[:, k:k + 1]
                for q in range(PACK_ROWS):
                    w = buf_ref[cur, pl.ds((k * tc + t0) * PACK_ROWS + q, ch, stride=PACK_ROWS), :]
                    lo = wk * pltpu.bitcast(w << 16, F32)
                    hi = wk * pltpu.bitcast(w & jnp.uint32(0xFFFF0000), F32)
                    acc[2 * q] = lo if acc[2 * q] is None else acc[2 * q] + lo
                    acc[2 * q + 1] = hi if acc[2 * q + 1] is None else acc[2 * q + 1] + hi
                    gather(posn_ref, t0 + k * PACK_ROWS + q, nxt)
            routed = jnp.concatenate(acc, axis=-1)
            o_ref[pl.ds(t0, ch), :] = _layer_norm(base_ref[pl.ds(t0, ch), :] + routed, g_ref[...], b_ref[...])
            return carry

        lax.fori_loop(0, tc // ch, chunk, 0)

        @pl.when(i == last)
        def _():
            oth = buf_ref.at[nxt]
            pltpu.make_async_copy(oth, oth, sem.at[nxt]).wait()

    @pl.when(i % 2 == 0)
    def _():
        reduce_tile(0, 1)

    @pl.when(i % 2 == 1)
    def _():
        reduce_tile(1, 0)


def _combine(pos_t, wgt, h_tiles, wsg, wsu, wsd, g, b, ys, tc):
    t = pos_t.shape[1]
    n_tiles = t // tc
    const = lambda shape: pl.BlockSpec(shape, lambda i: (0,) * len(shape))
    return pl.pallas_call(
        functools.partial(_combine_kernel, tc=tc),
        grid=(n_tiles,),
        in_specs=[pl.BlockSpec((TOP_K, tc), lambda i: (0, i), memory_space=pltpu.SMEM),
                  pl.BlockSpec((TOP_K, tc), lambda i: (0, jnp.minimum(i + 1, n_tiles - 1)), memory_space=pltpu.SMEM),
                  pl.BlockSpec((tc, TOP_K), lambda i: (i, 0)),
                  pl.BlockSpec((tc * ROW_TILES, LANES), lambda i: (i, 0)),
                  const((D_MODEL, D_EXPERT)), const((D_MODEL, D_EXPERT)), const((D_EXPERT, D_MODEL)),
                  const((1, D_MODEL)), const((1, D_MODEL)),
                  pl.BlockSpec(memory_space=pl.ANY)],
        out_specs=pl.BlockSpec((tc, D_MODEL), lambda i: (i, 0)),
        out_shape=jax.ShapeDtypeStruct((t, D_MODEL), F32),
        scratch_shapes=[pltpu.VMEM((2, TOP_K * tc * PACK_ROWS, LANES), U32), pltpu.VMEM((tc, D_MODEL), F32),
                        pltpu.SemaphoreType.DMA((2,))],
        compiler_params=pltpu.CompilerParams(dimension_semantics=("arbitrary",), vmem_limit_bytes=VMEM_LIMIT),
        name="moe_combine_ln2",
    )(pos_t, pos_t, wgt, h_tiles, wsg, wsu, wsd, g, b, ys)


def _pick(n, pref):
    t = min(n, pref)
    assert n % t == 0, (n, t)
    return t


def _layer(x, w_in, lq1, lk1, lq2, lk2, subw, conv_w, conv_b, dt_bias, a_log, d_skip, ssd_norm_w,
           w_br_attn, w_br_ssd, w_out, ln1_g, ln1_b, w_router, router_bias, w_eg, w_eu, w_ed,
           w_sg, w_su, w_sd, ln2_g, ln2_b, lam_init):
    b, s, d = x.shape
    t = b * s
    assert d == D_MODEL and s % SSD_CHUNK == 0
    x2d = x.reshape(t, d)

    o_q, o_z, o_xbc = 0, 3 * DA_WIDTH, 3 * DA_WIDTH + D_INNER
    o_dt = o_xbc + CONV_DIM
    o_g = o_dt + SSD_HEADS
    w_all = jnp.concatenate([w_in[:, o_xbc:o_dt], w_in[:, o_q:o_z], w_in[:, o_z:o_xbc],
                             w_in[:, o_g:o_g + 2 * D_MODEL]], axis=1).astype(BF16)
    w_dt = jnp.pad(w_in[:, o_dt:o_g], ((0, 0), (0, LANES - SSD_HEADS))).astype(BF16)
    xbc_blk, q_col0 = 0, CONV_DIM // LANES
    z_blk = (CONV_DIM + 3 * DA_WIDTH) // D_INNER
    gate_blk = (CONV_DIM + 3 * DA_WIDTH + D_INNER) // D_MODEL
    n_proj = w_all.shape[1]
    proj, dt_raw = _proj(x2d, w_all, w_dt, _pick(t, 2048), 2048)
    proj3 = proj.reshape(b, s, n_proj)

    tq = _pick(s, ATTN_Q_TILE)
    tk = tq // 2
    attn = _attention(proj3, q_col0, lq1[None], lk1[None], lq2[None], lk2[None], subw[None], lam_init, tq, tk)

    pad = LANES - SSD_HEADS
    y = _ssd(proj3, dt_raw.reshape(b, s, LANES), xbc_blk, z_blk, conv_w, conv_b[None],
             jnp.pad(dt_bias, (0, pad))[None], jnp.pad(a_log, (0, pad))[None],
             jnp.repeat(d_skip, SSD_HEAD_DIM)[None], ssd_norm_w[None], SSD_CHUNK)

    h1, h1p = _merge(attn.reshape(t, DA_WIDTH), y.reshape(t, D_INNER), proj, gate_blk, x2d, w_br_attn.astype(BF16),
                w_br_ssd.astype(BF16), w_out.astype(BF16), ln1_g[None], ln1_b[None], _pick(t, 512))

    tt = _pick(t, 512)
    idx_t, wgt_t, rank_t, cnt = _router(h1, w_router.T, router_bias[:, None], tt)
    blk = MOE_ROW_BLOCK
    n_blocks = (t * TOP_K) // blk + N_EXPERTS
    counts = cnt[:, 0].astype(I32)
    blocks_per_e = (counts + blk - 1) // blk
    blk_end = jnp.cumsum(blocks_per_e)
    blk_start = blk_end - blocks_per_e
    n_used = blk_end[-1:]
    jblk = jnp.minimum(jnp.arange(n_blocks, dtype=I32), n_used[0] - 1)
    blk_expert = jnp.minimum(jnp.sum((blk_end[None, :] <= jblk[:, None]).astype(I32), axis=1), N_EXPERTS - 1)
    blk_valid = jnp.clip(counts[blk_expert] - (jblk - blk_start[blk_expert]) * blk, 0, blk).astype(I32)
    base_col = (blk_start * blk).astype(F32)[:, None]
    pos_t = _positions(idx_t, rank_t, base_col, tt)

    xs = _dispatch(pos_t, h1p, n_blocks * blk, _pick(t, 2048))
    ys = _experts(blk_expert, blk_valid, n_used.astype(I32), xs, w_eg, w_eu, w_ed, blk)
    out = _combine(pos_t, wgt_t.T, h1, w_sg.astype(BF16), w_su.astype(BF16), w_sd.astype(BF16),
                   ln2_g[None], ln2_b[None], ys, _pick(t, 512))
    return out.reshape(b, s, d)


def kernel(x, w_in, lambda_q1, lambda_k1, lambda_q2, lambda_k2, attn_subln_w, conv_w, conv_b, dt_bias, a_log,
           d_skip, ssd_norm_w, w_br_attn, w_br_ssd, w_out, ln1_g, ln1_b, w_router, router_bias, w_exp_gate,
           w_exp_up, w_exp_down, w_sh_gate, w_sh_up, w_sh_down, ln2_g, ln2_b):
    h = x
    for layer in range(DEPTH):
        lam_init = 0.8 - 0.6 * math.exp(-0.3 * layer)
        h = _layer(h, w_in[layer], lambda_q1[layer], lambda_k1[layer], lambda_q2[layer], lambda_k2[layer],
                   attn_subln_w[layer], conv_w[layer], conv_b[layer], dt_bias[layer], a_log[layer],
                   d_skip[layer], ssd_norm_w[layer], w_br_attn[layer], w_br_ssd[layer], w_out[layer],
                   ln1_g[layer], ln1_b[layer], w_router[layer], router_bias[layer], w_exp_gate[layer],
                   w_exp_up[layer], w_exp_down[layer], w_sh_gate[layer], w_sh_up[layer], w_sh_down[layer],
                   ln2_g[layer], ln2_b[layer], lam_init)
    return h
```

```python
import functools
import math

import jax
import jax.numpy as jnp
from jax import lax
from jax.experimental import pallas as pl
from jax.experimental.pallas import tpu as pltpu

F32 = jnp.float32
BF16 = jnp.bfloat16
I32 = jnp.int32

D_MODEL = 1024
DEPTH = 1
DA_HEADS = 8
DA_HEAD_DIM = 64
DA_WIDTH = DA_HEADS * 2 * DA_HEAD_DIM
D_INNER = 2048
SSD_HEAD_DIM = 64
SSD_HEADS = D_INNER // SSD_HEAD_DIM
SSD_GROUPS = 4
SSD_STATE = 128
CONV_W = 4
CONV_DIM = D_INNER + 2 * SSD_GROUPS * SSD_STATE
SSD_CHUNK = 128
N_EXPERTS = 256
TOP_K = 8
N_EXPERT_GROUPS = 8
TOPK_GROUPS = 4
D_EXPERT = 256
ROUTED_SCALE = 2.5
DN_ALPHA = (2.0 * DEPTH) ** 0.25
LN_EPS = 1e-5

LANES = 128
SUBLANES = 8
ROW_TILES = D_MODEL // LANES
NEG = -1e30
MOE_ROW_BLOCK = 1024
VMEM_LIMIT = 56 * 1024 * 1024

NT_DIMS = (((1,), (1,)), ((), ()))


def _sigmoid(v):
    return 0.5 * jnp.tanh(0.5 * v) + 0.5


def _silu(v):
    hv = 0.5 * v
    return hv * jnp.tanh(hv) + hv


def _layer_norm(v, g, b):
    mu = jnp.mean(v, axis=-1, keepdims=True)
    d = v - mu
    var = jnp.mean(d * d, axis=-1, keepdims=True)
    return d * lax.rsqrt(var + LN_EPS) * g + b


def _load_token_tiles(ref, n_tok, row0=0):
    return jnp.concatenate(
        [ref[pl.ds(row0 + s, n_tok, stride=ROW_TILES), :] for s in range(ROW_TILES)], axis=-1)


def _store_token_tiles(ref, val, n_tok):
    for s in range(ROW_TILES):
        ref[pl.ds(s, n_tok, stride=ROW_TILES), :] = val[:, s * LANES:(s + 1) * LANES]


PACK_ROWS = ROW_TILES // 2
U32 = jnp.uint32


def _packed_rows(row0):
    return pl.ds(pl.multiple_of(row0, PACK_ROWS), PACK_ROWS)


def _store_packed_tokens(ref, val, n_tok):
    for q in range(PACK_ROWS):
        lo = pltpu.bitcast(val[:, (2 * q) * LANES:(2 * q + 1) * LANES].astype(BF16).astype(F32), U32)
        hi = pltpu.bitcast(val[:, (2 * q + 1) * LANES:(2 * q + 2) * LANES].astype(BF16).astype(F32), U32)
        ref[pl.ds(q, n_tok, stride=PACK_ROWS), :] = (lo >> 16) | hi


def _load_packed_tokens(ref, n_tok, tok0=0):
    chunks = []
    for q in range(PACK_ROWS):
        w = ref[pl.ds(tok0 * PACK_ROWS + q, n_tok, stride=PACK_ROWS), :]
        chunks.append(pltpu.bitcast(w << 16, F32))
        chunks.append(pltpu.bitcast(w & jnp.uint32(0xFFFF0000), F32))
    return jnp.concatenate(chunks, axis=-1)


def _split3(v):
    hi = v.astype(BF16)
    r1 = v - hi.astype(F32)
    mid = r1.astype(BF16)
    lo = (r1 - mid.astype(F32)).astype(BF16)
    return hi, mid, lo


def _proj_kernel(x_ref, w_ref, wdt_ref, o_ref, dt_ref, xb_ref):
    @pl.when(pl.program_id(1) == 0)
    def _():
        xb = x_ref[...].astype(BF16)
        xb_ref[...] = xb
        dt_ref[...] = jnp.dot(xb, wdt_ref[...], preferred_element_type=F32)

    o_ref[...] = jnp.dot(xb_ref[...], w_ref[...], preferred_element_type=F32).astype(o_ref.dtype)


def _proj(x2d, w, w_dt, tm, tn):
    t, k = x2d.shape
    n = w.shape[1]
    return pl.pallas_call(
        _proj_kernel,
        grid=(t // tm, n // tn),
        in_specs=[pl.BlockSpec((tm, k), lambda i, j: (i, 0)),
                  pl.BlockSpec((k, tn), lambda i, j: (0, j)),
                  pl.BlockSpec((k, LANES), lambda i, j: (0, 0))],
        out_specs=[pl.BlockSpec((tm, tn), lambda i, j: (i, j)),
                   pl.BlockSpec((tm, LANES), lambda i, j: (i, 0))],
        out_shape=[jax.ShapeDtypeStruct((t, n), BF16), jax.ShapeDtypeStruct((t, LANES), F32)],
        scratch_shapes=[pltpu.VMEM((tm, k), BF16)],
        compiler_params=pltpu.CompilerParams(dimension_semantics=("parallel", "arbitrary"),
                                             vmem_limit_bytes=VMEM_LIMIT),
        name="proj",
    )(x2d, w, w_dt)


ATTN_ROW_CHUNK = 64
ATTN_Q_TILE = 2048


def _attn_kernel(lq1_ref, lk1_ref, lq2_ref, lk2_ref, subw_ref, q_ref, k_ref, v_ref, o_ref,
                 qs_ref, m_ref, l_ref, acc_ref, *, tq, tk, tk_diag, lam_init):
    i = pl.program_id(2)
    rows = 2 * tq
    ch = ATTN_ROW_CHUNK
    q = q_ref[0].astype(F32) * (DA_HEAD_DIM ** -0.5 * math.log2(math.e))
    lane = lax.broadcasted_iota(I32, (tq, LANES), 1)
    qs_ref[0:tq, :] = jnp.where(lane < DA_HEAD_DIM, q, 0.0).astype(BF16)
    qs_ref[tq:rows, :] = jnp.where(lane >= DA_HEAD_DIM, q, 0.0).astype(BF16)
    m_ref[...] = jnp.full((rows, LANES), NEG, F32)
    l_ref[...] = jnp.zeros((rows, LANES), F32)
    acc_ref[...] = jnp.zeros((rows, LANES), F32)
    qs = qs_ref[...]

    def step(j, q0, diagonal, tk):
        start = pl.multiple_of(j * tk, tk)
        kb = k_ref[0, pl.ds(start, tk), :]
        vb = v_ref[0, pl.ds(start, tk), :]
        nq = tq - q0
        q_sub = qs if q0 == 0 else jnp.concatenate([qs[q0:tq, :], qs[tq + q0:rows, :]], axis=0)
        s = lax.dot_general(q_sub, kb, NT_DIMS, preferred_element_type=F32)
        ps, alphas = [], []
        for c in range(2 * nq // ch):
            ql = q0 + (c * ch) % nq
            g0 = (c * ch) // nq * tq + ql
            sc = s[c * ch:(c + 1) * ch, :]
            if diagonal and ql < q0 + tk:
                row = (ql - q0) + lax.broadcasted_iota(I32, (ch, tk), 0)
                col = lax.broadcasted_iota(I32, (ch, tk), 1)
                sc = jnp.where(col <= row, sc, NEG)
            m_prev = m_ref[g0:g0 + ch, :]
            m_new = jnp.maximum(m_prev, jnp.max(sc, axis=-1, keepdims=True))
            alpha = jnp.exp2(m_prev - m_new)
            p = jnp.exp2(sc - jnp.concatenate([m_new] * (tk // LANES), axis=1))
            psum = p[:, 0:LANES]
            for t in range(1, tk // LANES):
                psum = psum + p[:, t * LANES:(t + 1) * LANES]
            l_ref[g0:g0 + ch, :] = alpha * l_ref[g0:g0 + ch, :] + psum
            m_ref[g0:g0 + ch, :] = m_new
            ps.append(p.astype(BF16))
            alphas.append(alpha)
        pv = jnp.dot(jnp.concatenate(ps, axis=0), vb, preferred_element_type=F32)
        alpha_all = jnp.concatenate(alphas, axis=0)
        for mp in range(2):
            dst = slice(mp * tq + q0, (mp + 1) * tq)
            src = slice(mp * nq, (mp + 1) * nq)
            acc_ref[dst, :] = alpha_all[src, :] * acc_ref[dst, :] + pv[src, :]

    per_tile = tq // tk

    def full_block_pair(jj, carry):
        step(2 * jj, 0, False, tk)
        step(2 * jj + 1, 0, False, tk)
        return carry

    lax.fori_loop(0, i * (per_tile // 2), full_block_pair, 0)
    for d in range(tq // tk_diag):
        step(i * (tq // tk_diag) + d, d * tk_diag, True, tk_diag)
    o = acc_ref[...] / jnp.sum(l_ref[...], axis=-1, keepdims=True)
    lam = (jnp.exp(jnp.sum(lq1_ref[...] * lk1_ref[...], axis=-1, keepdims=True))
           - jnp.exp(jnp.sum(lq2_ref[...] * lk2_ref[...], axis=-1, keepdims=True)) + lam_init)
    d = o[0:tq, :] - lam * o[tq:rows, :]
    ms = jnp.mean(d * d, axis=-1, keepdims=True)
    o_ref[0] = (d * lax.rsqrt(ms + LN_EPS) * subw_ref[...] * (1.0 - lam_init)).astype(o_ref.dtype)


def _attention(proj, col0, lq1, lk1, lq2, lk2, subw, lam_init, tq, tk, tk_diag):
    b, s, _ = proj.shape
    vec = pl.BlockSpec((1, DA_HEAD_DIM), lambda bi, h, i: (0, 0))
    return pl.pallas_call(
        functools.partial(_attn_kernel, tq=tq, tk=tk, tk_diag=tk_diag, lam_init=lam_init),
        grid=(b, DA_HEADS, s // tq),
        in_specs=[vec, vec, vec, vec,
                  pl.BlockSpec((1, LANES), lambda bi, h, i: (0, 0)),
                  pl.BlockSpec((1, tq, LANES), lambda bi, h, i: (bi, i, col0 + h)),
                  pl.BlockSpec((1, s, LANES), lambda bi, h, i: (bi, 0, col0 + DA_HEADS + h)),
                  pl.BlockSpec((1, s, LANES), lambda bi, h, i: (bi, 0, col0 + 2 * DA_HEADS + h))],
        out_specs=pl.BlockSpec((1, tq, LANES), lambda bi, h, i: (bi, i, h)),
        out_shape=jax.ShapeDtypeStruct((b, s, DA_WIDTH), BF16),
        scratch_shapes=[pltpu.VMEM((2 * tq, LANES), BF16), pltpu.VMEM((2 * tq, LANES), F32),
                        pltpu.VMEM((2 * tq, LANES), F32), pltpu.VMEM((2 * tq, LANES), F32)],
        compiler_params=pltpu.CompilerParams(dimension_semantics=("parallel", "parallel", "arbitrary"),
                                             vmem_limit_bytes=VMEM_LIMIT),
        name="diff_attn",
    )(lq1, lk1, lq2, lk2, subw, proj, proj, proj)


CONV_HALO = 16


def _ssd_kernel(xbc_ref, dt_ref, z_ref, shift_ref, convw_ref, convb_ref, dtb_ref, alog_ref, dskip_ref, normw_ref,
                y_ref, ext_ref, state_ref, *, L):
    c = pl.program_id(1)
    halo = CONV_HALO

    @pl.when(c == 0)
    def _():
        ext_ref[0:halo, :] = jnp.zeros((halo, CONV_DIM), BF16)
        state_ref[...] = jnp.zeros(state_ref.shape, F32)

    ext_ref[halo:halo + L, :] = xbc_ref[0]
    delayed = jnp.dot(shift_ref[...], ext_ref[...], preferred_element_type=F32)
    acc = convb_ref[...] + convw_ref[CONV_W - 1:CONV_W, :] * xbc_ref[0].astype(F32)
    for d in range(1, CONV_W):
        acc = acc + convw_ref[CONV_W - 1 - d:CONV_W - d, :] * delayed[(d - 1) * L:d * L, :]
    ext_ref[0:halo, :] = ext_ref[L:L + halo, :]
    xbc = _silu(acc)
    xs = xbc[:, 0:D_INNER]
    bm = xbc[:, D_INNER:D_INNER + SSD_GROUPS * SSD_STATE]
    cm = xbc[:, D_INNER + SSD_GROUPS * SSD_STATE:CONV_DIM]

    dt_in = dt_ref[0] + dtb_ref[...]
    dt = jnp.maximum(dt_in, 0.0) + jnp.log1p(jnp.exp(-jnp.abs(dt_in)))
    a = -jnp.exp(alog_ref[...]) * math.log2(math.e)
    da = dt * a
    row = lax.broadcasted_iota(I32, (L, L), 0)
    col = lax.broadcasted_iota(I32, (L, L), 1)
    tri = row >= col
    tril = jnp.where(tri, 1.0, 0.0).astype(BF16)
    hi, mid, lo = _split3(da)
    a_cum = (jnp.dot(tril, hi, preferred_element_type=F32) + jnp.dot(tril, mid, preferred_element_type=F32)
             + jnp.dot(tril, lo, preferred_element_type=F32))
    a_last = a_cum[L - 1:L, :]
    a_cum_t = a_cum.T
    dt_t = dt.T
    dtw_t = (dt * jnp.exp2(a_last - a_cum)).T
    ea = jnp.exp2(a_cum)
    e_last = jnp.exp2(a_last)
    lane = lax.broadcasted_iota(I32, (1, LANES), 1)
    first_half = lane < SSD_HEAD_DIM

    heads_per_group = SSD_HEADS // SSD_GROUPS
    pairs_per_group = heads_per_group // 2
    ys = []
    for g in range(SSD_GROUPS):
        bg = bm[:, g * SSD_STATE:(g + 1) * SSD_STATE]
        cg = cm[:, g * SSD_STATE:(g + 1) * SSD_STATE]
        cb = lax.dot_general(cg.astype(BF16), bg.astype(BF16), NT_DIMS, preferred_element_type=F32)
        bg_t = bg.T
        for pp in range(pairs_per_group):
            pair = g * pairs_per_group + pp
            x_pair = xs[:, pair * LANES:(pair + 1) * LANES].astype(BF16)
            st = state_ref[pair]
            rhs = jnp.concatenate([x_pair, st.astype(BF16)], axis=0)
            y_h, s_h, e_h = [], [], []
            for hl in range(2):
                h = g * heads_per_group + 2 * pp + hl
                seg = a_cum[:, h:h + 1] - a_cum_t[h:h + 1, :]
                decay = jnp.exp2(jnp.where(tri, seg, NEG))
                m_h = (cb * decay * dt_t[h:h + 1, :]).astype(BF16)
                c_h = (cg * ea[:, h:h + 1]).astype(BF16)
                lhs = jnp.concatenate([m_h, c_h], axis=1)
                y_h.append(jnp.dot(lhs, rhs, preferred_element_type=F32))
                b_s = (bg_t * dtw_t[h:h + 1, :]).astype(BF16)
                s_h.append(jnp.dot(b_s, x_pair, preferred_element_type=F32))
                e_h.append(e_last[:, h:h + 1])
            ys.append(jnp.where(first_half, y_h[0], y_h[1]))
            state_ref[pair] = (st * jnp.where(first_half, e_h[0], e_h[1])
                               + jnp.where(first_half, s_h[0], s_h[1]))
    y = jnp.concatenate(ys, axis=1)
    y = y + dskip_ref[...] * xs
    y = y * _silu(z_ref[0].astype(F32))
    gw = D_INNER // SSD_GROUPS
    outs = []
    for g in range(SSD_GROUPS):
        yg = y[:, g * gw:(g + 1) * gw]
        ms = jnp.mean(yg * yg, axis=-1, keepdims=True)
        outs.append(yg * lax.rsqrt(ms + LN_EPS) * normw_ref[:, g * gw:(g + 1) * gw])
    y_ref[0] = jnp.concatenate(outs, axis=1).astype(y_ref.dtype)


def _ssd(proj, dt_raw, xbc_blk, z_blk, conv_w, conv_b, dt_bias, a_log, d_skip, norm_w, L):
    b, s, _ = proj.shape
    const = lambda shape: pl.BlockSpec(shape, lambda bi, c: (0,) * len(shape))
    r = jnp.arange((CONV_W - 1) * L, dtype=I32)[:, None]
    shift = (jnp.arange(L + CONV_HALO, dtype=I32)[None, :] == CONV_HALO + r % L - (r // L + 1)).astype(BF16)
    return pl.pallas_call(
        functools.partial(_ssd_kernel, L=L),
        grid=(b, s // L),
        in_specs=[pl.BlockSpec((1, L, CONV_DIM), lambda bi, c: (bi, c, xbc_blk)),
                  pl.BlockSpec((1, L, LANES), lambda bi, c: (bi, c, 0)),
                  pl.BlockSpec((1, L, D_INNER), lambda bi, c: (bi, c, z_blk)),
                  const(((CONV_W - 1) * L, L + CONV_HALO)), const((CONV_W, CONV_DIM)), const((1, CONV_DIM)), const((1, LANES)), const((1, LANES)),
                  const((1, D_INNER)), const((1, D_INNER))],
        out_specs=pl.BlockSpec((1, L, D_INNER), lambda bi, c: (bi, c, 0)),
        out_shape=jax.ShapeDtypeStruct((b, s, D_INNER), BF16),
        scratch_shapes=[pltpu.VMEM((L + CONV_HALO, CONV_DIM), BF16),
                        pltpu.VMEM((SSD_HEADS // 2, SSD_STATE, LANES), F32)],
        compiler_params=pltpu.CompilerParams(dimension_semantics=("parallel", "arbitrary"),
                                             vmem_limit_bytes=VMEM_LIMIT),
        name="ssd",
    )(proj, dt_raw, proj, shift, conv_w, conv_b, dt_bias, a_log, d_skip, norm_w)


def _merge_kernel(attn_ref, y_ref, ga_ref, gs_ref, x_ref, wa_ref, ws_ref, wo_ref, g_ref, b_ref, o_ref, op_ref,
                  *, tm):
    a = jnp.dot(attn_ref[...], wa_ref[...], preferred_element_type=F32)
    s = jnp.dot(y_ref[...], ws_ref[...], preferred_element_type=F32)
    merged = _sigmoid(ga_ref[...].astype(F32)) * a + _sigmoid(gs_ref[...].astype(F32)) * s
    out = jnp.dot(merged.astype(BF16), wo_ref[...], preferred_element_type=F32)
    h = _layer_norm(DN_ALPHA * x_ref[...] + out, g_ref[...], b_ref[...])
    _store_token_tiles(o_ref, h, tm)
    _store_packed_tokens(op_ref, h, tm)


def _merge(attn2d, y2d, proj2d, gate_blk, x2d, wa, ws, wo, g, b, tm):
    t = x2d.shape[0]
    const = lambda shape: pl.BlockSpec(shape, lambda i: (0,) * len(shape))
    return pl.pallas_call(
        functools.partial(_merge_kernel, tm=tm),
        grid=(t // tm,),
        in_specs=[pl.BlockSpec((tm, DA_WIDTH), lambda i: (i, 0)),
                  pl.BlockSpec((tm, D_INNER), lambda i: (i, 0)),
                  pl.BlockSpec((tm, D_MODEL), lambda i: (i, gate_blk)),
                  pl.BlockSpec((tm, D_MODEL), lambda i: (i, gate_blk + 1)),
                  pl.BlockSpec((tm, D_MODEL), lambda i: (i, 0)),
                  const((DA_WIDTH, D_MODEL)), const((D_INNER, D_MODEL)), const((D_MODEL, D_MODEL)),
                  const((1, D_MODEL)), const((1, D_MODEL))],
        out_specs=[pl.BlockSpec((tm * ROW_TILES, LANES), lambda i: (i, 0)),
                   pl.BlockSpec((tm * PACK_ROWS, LANES), lambda i: (i, 0))],
        out_shape=[jax.ShapeDtypeStruct((t * ROW_TILES, LANES), F32),
                   jax.ShapeDtypeStruct((t * PACK_ROWS, LANES), U32)],
        compiler_params=pltpu.CompilerParams(dimension_semantics=("parallel",), vmem_limit_bytes=VMEM_LIMIT),
        name="merge_ln1",
    )(attn2d, y2d, proj2d, proj2d, x2d, wa, ws, wo, g, b)


def _router_kernel(h_ref, w_ref, bias_ref, idx_ref, wgt_ref, rank_ref, cnt_ref, carry_ref, *, tt):
    i = pl.program_id(0)

    @pl.when(i == 0)
    def _():
        carry_ref[...] = jnp.zeros(carry_ref.shape, F32)

    h = _load_token_tiles(h_ref, tt)
    h_hi = h.astype(BF16)
    h_lo = (h - h_hi.astype(F32)).astype(BF16)
    w = w_ref[...]
    w_hi = w.astype(BF16)
    w_lo = (w - w_hi.astype(F32)).astype(BF16)
    logits = (lax.dot_general(w_hi, h_hi, NT_DIMS, preferred_element_type=F32)
              + lax.dot_general(w_hi, h_lo, NT_DIMS, preferred_element_type=F32)
              + lax.dot_general(w_lo, h_hi, NT_DIMS, preferred_element_type=F32))
    scores = _sigmoid(logits)
    choice = scores + bias_ref[...]

    gsz = N_EXPERTS // N_EXPERT_GROUPS
    iota_g = lax.broadcasted_iota(I32, (gsz, tt), 0)
    rows = []
    for g in range(N_EXPERT_GROUPS):
        cg = choice[g * gsz:(g + 1) * gsz, :]
        m1 = jnp.max(cg, axis=0, keepdims=True)
        i1 = jnp.min(jnp.where(cg == m1, iota_g, gsz), axis=0, keepdims=True)
        m2 = jnp.max(jnp.where(iota_g == i1, NEG, cg), axis=0, keepdims=True)
        rows.append(m1 + m2)
    gscore = jnp.concatenate(rows, axis=0)

    iota_grp = lax.broadcasted_iota(I32, (N_EXPERT_GROUPS, tt), 0)
    sel = jnp.zeros((N_EXPERT_GROUPS, tt), F32)
    cur = gscore
    for _ in range(TOPK_GROUPS):
        m = jnp.max(cur, axis=0, keepdims=True)
        ig = jnp.min(jnp.where(cur == m, iota_grp, N_EXPERT_GROUPS), axis=0, keepdims=True)
        hit = iota_grp == ig
        sel = jnp.where(hit, 1.0, sel)
        cur = jnp.where(hit, NEG, cur)
    sel_e = jnp.concatenate([jnp.broadcast_to(sel[g:g + 1, :], (gsz, tt)) for g in range(N_EXPERT_GROUPS)], axis=0)
    cur = jnp.where(sel_e > 0.5, choice, NEG)

    iota_e = lax.broadcasted_iota(I32, (N_EXPERTS, tt), 0)
    idx_rows, w_rows, hits = [], [], []
    for _ in range(TOP_K):
        m = jnp.max(cur, axis=0, keepdims=True)
        ik = jnp.min(jnp.where(cur == m, iota_e, N_EXPERTS), axis=0, keepdims=True)
        hit = iota_e == ik
        w_rows.append(jnp.sum(jnp.where(hit, scores, 0.0), axis=0, keepdims=True))
        cur = jnp.where(hit, NEG, cur)
        idx_rows.append(ik)
        hits.append(hit)
    wv = jnp.concatenate(w_rows, axis=0)
    wv = wv / jnp.sum(wv, axis=0, keepdims=True) * ROUTED_SCALE
    idx_ref[...] = jnp.concatenate(idx_rows, axis=0)
    wgt_ref[...] = wv

    onehot = jnp.zeros((N_EXPERTS, tt), F32)
    for hit in hits:
        onehot = jnp.where(hit, 1.0, onehot)
    onehot_b = onehot.astype(BF16)
    ti = lax.broadcasted_iota(I32, (tt, tt), 0)
    tj = lax.broadcasted_iota(I32, (tt, tt), 1)
    before = jnp.where(ti < tj, 1.0, 0.0).astype(BF16)
    prefix = jnp.dot(onehot_b, before, preferred_element_type=F32) + carry_ref[...]
    rank_rows = [jnp.sum(jnp.where(hit, prefix, 0.0), axis=0, keepdims=True) for hit in hits]
    rank_ref[...] = jnp.concatenate(rank_rows, axis=0).astype(I32)
    total = jnp.dot(onehot_b, jnp.ones((tt, tt), BF16), preferred_element_type=F32)
    new_carry = carry_ref[...] + total
    carry_ref[...] = new_carry
    cnt_ref[...] = new_carry[:, 0:LANES]


def _router(h_tiles, w_router_t, bias_col, tt):
    t = h_tiles.shape[0] // ROW_TILES
    kt = pl.BlockSpec((TOP_K, tt), lambda i: (0, i))
    return pl.pallas_call(
        functools.partial(_router_kernel, tt=tt),
        grid=(t // tt,),
        in_specs=[pl.BlockSpec((tt * ROW_TILES, LANES), lambda i: (i, 0)),
                  pl.BlockSpec((N_EXPERTS, D_MODEL), lambda i: (0, 0)),
                  pl.BlockSpec((N_EXPERTS, 1), lambda i: (0, 0))],
        out_specs=[kt, kt, kt, pl.BlockSpec((N_EXPERTS, LANES), lambda i: (0, 0))],
        out_shape=[jax.ShapeDtypeStruct((TOP_K, t), I32), jax.ShapeDtypeStruct((TOP_K, t), F32),
                   jax.ShapeDtypeStruct((TOP_K, t), I32), jax.ShapeDtypeStruct((N_EXPERTS, LANES), F32)],
        scratch_shapes=[pltpu.VMEM((N_EXPERTS, tt), F32)],
        compiler_params=pltpu.CompilerParams(dimension_semantics=("arbitrary",), vmem_limit_bytes=VMEM_LIMIT),
        name="moe_router",
    )(h_tiles, w_router_t, bias_col)


def _pos_kernel(idx_ref, rank_ref, base_ref, pos_ref, *, tt):
    iota_e = lax.broadcasted_iota(I32, (N_EXPERTS, tt), 0)
    base = base_ref[...]
    rows = []
    for k in range(TOP_K):
        hit = iota_e == idx_ref[k:k + 1, :]
        rows.append(jnp.sum(jnp.where(hit, base, 0.0), axis=0, keepdims=True))
    pos_ref[...] = jnp.concatenate(rows, axis=0).astype(I32) + rank_ref[...]


def _positions(idx_t, rank_t, base_col, tt):
    t = idx_t.shape[1]
    kt = pl.BlockSpec((TOP_K, tt), lambda i: (0, i))
    return pl.pallas_call(
        functools.partial(_pos_kernel, tt=tt),
        grid=(t // tt,),
        in_specs=[kt, kt, pl.BlockSpec((N_EXPERTS, 1), lambda i: (0, 0))],
        out_specs=kt,
        out_shape=jax.ShapeDtypeStruct((TOP_K, t), I32),
        compiler_params=pltpu.CompilerParams(dimension_semantics=("parallel",)),
        name="moe_positions",
    )(idx_t, rank_t, base_col)


def _dispatch_kernel(pos_ref, h_ref, xs_ref, sem, *, tt):
    def issue(t, carry):
        for k in range(TOP_K):
            pltpu.make_async_copy(h_ref.at[_packed_rows(t * PACK_ROWS), :],
                                  xs_ref.at[_packed_rows(pos_ref[k, t] * PACK_ROWS), :], sem).start(priority=k % 2)
        return carry

    lax.fori_loop(0, tt, issue, 0)
    done = xs_ref.at[pl.ds(0, tt * TOP_K * PACK_ROWS), :]
    pltpu.make_async_copy(done, done, sem).wait()


def _dispatch(pos_t, h_packed, n_rows, tt):
    t = pos_t.shape[1]
    return pl.pallas_call(
        functools.partial(_dispatch_kernel, tt=tt),
        grid=(t // tt,),
        in_specs=[pl.BlockSpec((TOP_K, tt), lambda i: (0, i), memory_space=pltpu.SMEM),
                  pl.BlockSpec((tt * PACK_ROWS, LANES), lambda i: (i, 0))],
        out_specs=pl.BlockSpec(memory_space=pl.ANY),
        out_shape=jax.ShapeDtypeStruct((n_rows * PACK_ROWS, LANES), U32),
        scratch_shapes=[pltpu.SemaphoreType.DMA(())],
        compiler_params=pltpu.CompilerParams(dimension_semantics=("arbitrary",), has_side_effects=True),
        name="moe_dispatch",
    )(pos_t, h_packed)


def _expert_kernel(be_ref, nv_ref, nu_ref, xs_ref, wg_ref, wu_ref, wd_ref, ys_ref, wgb_ref, wub_ref, wdb_ref,
                   *, blk):
    j = pl.program_id(0)

    @pl.when(j < nu_ref[0])
    def _():
        @pl.when((j == 0) | (be_ref[j] != be_ref[jnp.maximum(j - 1, 0)]))
        def _():
            wgb_ref[...] = wg_ref[0].astype(BF16)
            wub_ref[...] = wu_ref[0].astype(BF16)
            wdb_ref[...] = wd_ref[0].astype(BF16)

        x = _load_packed_tokens(xs_ref, blk).astype(BF16)
        g = jnp.dot(x, wgb_ref[...], preferred_element_type=F32)
        u = jnp.dot(x, wub_ref[...], preferred_element_type=F32)
        valid = lax.broadcasted_iota(I32, (blk, 1), 0) < nv_ref[j]
        hmid = jnp.where(valid, _silu(g) * u, 0.0).astype(BF16)
        _store_packed_tokens(ys_ref, jnp.dot(hmid, wdb_ref[...], preferred_element_type=F32), blk)


def _experts(blk_expert, blk_valid, n_used, xs, wg, wu, wd, blk):
    n_blocks = blk_expert.shape[0]
    rows = lambda j, be, nv, nu: (jnp.minimum(j, nu[0] - 1), 0)
    wsel = lambda j, be, nv, nu: (be[j], 0, 0)
    return pl.pallas_call(
        functools.partial(_expert_kernel, blk=blk),
        grid_spec=pltpu.PrefetchScalarGridSpec(
            num_scalar_prefetch=3,
            grid=(n_blocks,),
            in_specs=[pl.BlockSpec((blk * PACK_ROWS, LANES), rows),
                      pl.BlockSpec((1, D_MODEL, D_EXPERT), wsel),
                      pl.BlockSpec((1, D_MODEL, D_EXPERT), wsel),
                      pl.BlockSpec((1, D_EXPERT, D_MODEL), wsel)],
            out_specs=pl.BlockSpec((blk * PACK_ROWS, LANES), rows),
            scratch_shapes=[pltpu.VMEM((D_MODEL, D_EXPERT), BF16), pltpu.VMEM((D_MODEL, D_EXPERT), BF16),
                            pltpu.VMEM((D_EXPERT, D_MODEL), BF16)]),
        out_shape=jax.ShapeDtypeStruct(xs.shape, U32),
        compiler_params=pltpu.CompilerParams(dimension_semantics=("arbitrary",), vmem_limit_bytes=VMEM_LIMIT),
        name="moe_experts",
    )(blk_expert, blk_valid, n_used, xs, wg, wu, wd)


COMBINE_CHUNK = TOP_K * PACK_ROWS


def _combine_kernel(pos_ref, posn_ref, wgt_ref, h_ref, wsg_ref, wsu_ref, wsd_ref, g_ref, b_ref, ys_ref, o_ref,
                    buf_ref, base_ref, sem, *, tc):
    i = pl.program_id(0)
    last = pl.num_programs(0) - 1
    ch = COMBINE_CHUNK

    def gather(p_ref, t, dst):
        for k in range(TOP_K):
            pltpu.make_async_copy(ys_ref.at[_packed_rows(p_ref[k, t] * PACK_ROWS), :],
                                  buf_ref.at[dst, _packed_rows((k * tc + t) * PACK_ROWS), :],
                                  sem.at[dst]).start(priority=k % 2)

    @pl.when(i == 0)
    def _():
        def first(t, carry):
            gather(pos_ref, t, 0)
            return carry

        lax.fori_loop(0, tc, first, 0)

    h = _load_token_tiles(h_ref, tc)
    hb = h.astype(BF16)
    g = jnp.dot(hb, wsg_ref[...], preferred_element_type=F32)
    u = jnp.dot(hb, wsu_ref[...], preferred_element_type=F32)
    shared = jnp.dot((_silu(g) * u).astype(BF16), wsd_ref[...], preferred_element_type=F32)
    base_ref[...] = DN_ALPHA * h + shared

    def reduce_tile(cur, nxt):
        done = buf_ref.at[cur]
        pltpu.make_async_copy(done, done, sem.at[cur]).wait()

        def chunk(c, carry):
            t0 = pl.multiple_of(c * ch, ch)
            wgt = wgt_ref[pl.ds(t0, ch), :]
            acc = [None] * ROW_TILES
            for k in range(TOP_K):
                wk = wgt[:, k:k + 1]
                for q in range(PACK_ROWS):
                    w = buf_ref[cur, pl.ds((k * tc + t0) * PACK_ROWS + q, ch, stride=PACK_ROWS), :]
                    lo = wk * pltpu.bitcast(w << 16, F32)
                    hi = wk * pltpu.bitcast(w & jnp.uint32(0xFFFF0000), F32)
                    acc[2 * q] = lo if acc[2 * q] is None else acc[2 * q] + lo
                    acc[2 * q + 1] = hi if acc[2 * q + 1] is None else acc[2 * q + 1] + hi
                    gather(posn_ref, t0 + k * PACK_ROWS + q, nxt)
            routed = jnp.concatenate(acc, axis=-1)
            o_ref[pl.ds(t0, ch), :] = _layer_norm(base_ref[pl.ds(t0, ch), :] + routed, g_ref[...], b_ref[...])
            return carry

        lax.fori_loop(0, tc // ch, chunk, 0)

        @pl.when(i == last)
        def _():
            oth = buf_ref.at[nxt]
            pltpu.make_async_copy(oth, oth, sem.at[nxt]).wait()

    @pl.when(i % 2 == 0)
    def _():
        reduce_tile(0, 1)

    @pl.when(i % 2 == 1)
    def _():
        reduce_tile(1, 0)


def _combine(pos_t, wgt, h_tiles, wsg, wsu, wsd, g, b, ys, tc):
    t = pos_t.shape[1]
    n_tiles = t // tc
    const = lambda shape: pl.BlockSpec(shape, lambda i: (0,) * len(shape))
    return pl.pallas_call(
        functools.partial(_combine_kernel, tc=tc),
        grid=(n_tiles,),
        in_specs=[pl.BlockSpec((TOP_K, tc), lambda i: (0, i), memory_space=pltpu.SMEM),
                  pl.BlockSpec((TOP_K, tc), lambda i: (0, jnp.minimum(i + 1, n_tiles - 1)), memory_space=pltpu.SMEM),
                  pl.BlockSpec((tc, TOP_K), lambda i: (i, 0)),
                  pl.BlockSpec((tc * ROW_TILES, LANES), lambda i: (i, 0)),
                  const((D_MODEL, D_EXPERT)), const((D_MODEL, D_EXPERT)), const((D_EXPERT, D_MODEL)),
                  const((1, D_MODEL)), const((1, D_MODEL)),
                  pl.BlockSpec(memory_space=pl.ANY)],
        out_specs=pl.BlockSpec((tc, D_MODEL), lambda i: (i, 0)),
        out_shape=jax.ShapeDtypeStruct((t, D_MODEL), F32),
        scratch_shapes=[pltpu.VMEM((2, TOP_K * tc * PACK_ROWS, LANES), U32), pltpu.VMEM((tc, D_MODEL), F32),
                        pltpu.SemaphoreType.DMA((2,))],
        compiler_params=pltpu.CompilerParams(dimension_semantics=("arbitrary",), vmem_limit_bytes=VMEM_LIMIT),
        name="moe_combine_ln2",
    )(pos_t, pos_t, wgt, h_tiles, wsg, wsu, wsd, g, b, ys)


def _pick(n, pref):
    t = min(n, pref)
    assert n % t == 0, (n, t)
    return t


def _layer(x, w_in, lq1, lk1, lq2, lk2, subw, conv_w, conv_b, dt_bias, a_log, d_skip, ssd_norm_w,
           w_br_attn, w_br_ssd, w_out, ln1_g, ln1_b, w_router, router_bias, w_eg, w_eu, w_ed,
           w_sg, w_su, w_sd, ln2_g, ln2_b, lam_init):
    b, s, d = x.shape
    t = b * s
    assert d == D_MODEL and s % SSD_CHUNK == 0
    x2d = x.reshape(t, d)

    o_q, o_z, o_xbc = 0, 3 * DA_WIDTH, 3 * DA_WIDTH + D_INNER
    o_dt = o_xbc + CONV_DIM
    o_g = o_dt + SSD_HEADS
    w_all = jnp.concatenate([w_in[:, o_xbc:o_dt], w_in[:, o_q:o_z], w_in[:, o_z:o_xbc],
                             w_in[:, o_g:o_g + 2 * D_MODEL]], axis=1).astype(BF16)
    w_dt = jnp.pad(w_in[:, o_dt:o_g], ((0, 0), (0, LANES - SSD_HEADS))).astype(BF16)
    xbc_blk, q_col0 = 0, CONV_DIM // LANES
    z_blk = (CONV_DIM + 3 * DA_WIDTH) // D_INNER
    gate_blk = (CONV_DIM + 3 * DA_WIDTH + D_INNER) // D_MODEL
    n_proj = w_all.shape[1]
    proj, dt_raw = _proj(x2d, w_all, w_dt, _pick(t, 2048), 2048)
    proj3 = proj.reshape(b, s, n_proj)

    tq = _pick(s, ATTN_Q_TILE)
    tk = tq // 2
    attn = _attention(proj3, q_col0, lq1[None], lk1[None], lq2[None], lk2[None], subw[None], lam_init, tq, tk,
                      max(tq // 8, LANES))

    pad = LANES - SSD_HEADS
    y = _ssd(proj3, dt_raw.reshape(b, s, LANES), xbc_blk, z_blk, conv_w, conv_b[None],
             jnp.pad(dt_bias, (0, pad))[None], jnp.pad(a_log, (0, pad))[None],
             jnp.repeat(d_skip, SSD_HEAD_DIM)[None], ssd_norm_w[None], SSD_CHUNK)

    h1, h1p = _merge(attn.reshape(t, DA_WIDTH), y.reshape(t, D_INNER), proj, gate_blk, x2d, w_br_attn.astype(BF16),
                w_br_ssd.astype(BF16), w_out.astype(BF16), ln1_g[None], ln1_b[None], _pick(t, 512))

    tt = _pick(t, 512)
    idx_t, wgt_t, rank_t, cnt = _router(h1, w_router.T, router_bias[:, None], tt)
    blk = MOE_ROW_BLOCK
    n_blocks = (t * TOP_K) // blk + N_EXPERTS
    counts = cnt[:, 0].astype(I32)
    blocks_per_e = (counts + blk - 1) // blk
    blk_end = jnp.cumsum(blocks_per_e)
    blk_start = blk_end - blocks_per_e
    n_used = blk_end[-1:]
    jblk = jnp.minimum(jnp.arange(n_blocks, dtype=I32), n_used[0] - 1)
    blk_expert = jnp.minimum(jnp.sum((blk_end[None, :] <= jblk[:, None]).astype(I32), axis=1), N_EXPERTS - 1)
    blk_valid = jnp.clip(counts[blk_expert] - (jblk - blk_start[blk_expert]) * blk, 0, blk).astype(I32)
    base_col = (blk_start * blk).astype(F32)[:, None]
    pos_t = _positions(idx_t, rank_t, base_col, tt)

    xs = _dispatch(pos_t, h1p, n_blocks * blk, _pick(t, 2048))
    ys = _experts(blk_expert, blk_valid, n_used.astype(I32), xs, w_eg, w_eu, w_ed, blk)
    out = _combine(pos_t, wgt_t.T, h1, w_sg.astype(BF16), w_su.astype(BF16), w_sd.astype(BF16),
                   ln2_g[None], ln2_b[None], ys, _pick(t, 512))
    return out.reshape(b, s, d)


def kernel(x, w_in, lambda_q1, lambda_k1, lambda_q2, lambda_k2, attn_subln_w, conv_w, conv_b, dt_bias, a_log,
           d_skip, ssd_norm_w, w_br_attn, w_br_ssd, w_out, ln1_g, ln1_b, w_router, router_bias, w_exp_gate,
           w_exp_up, w_exp_down, w_sh_gate, w_sh_up, w_sh_down, ln2_g, ln2_b):
    h = x
    for layer in range(DEPTH):
        lam_init = 0.8 - 0.6 * math.exp(-0.3 * layer)
        h = _layer(h, w_in[layer], lambda_q1[layer], lambda_k1[layer], lambda_q2[layer], lambda_k2[layer],
                   attn_subln_w[layer], conv_w[layer], conv_b[layer], dt_bias[layer], a_log[layer],
                   d_skip[layer], ssd_norm_w[layer], w_br_attn[layer], w_br_ssd[layer], w_out[layer],
                   ln1_g[layer], ln1_b[layer], w_router[layer], router_bias[layer], w_exp_gate[layer],
                   w_exp_up[layer], w_exp_down[layer], w_sh_gate[layer], w_sh_up[layer], w_sh_down[layer],
                   ln2_g[layer], ln2_b[layer], lam_init)
    return h
```
